```python
import math
import jax, jax.numpy as jnp
from jax import lax
import numpy as np

D_MODEL = 2048
BATCH = 1
SEQ = 16384
DEPTH = 4

CHUNK = 128
A_GROUPS = 8
A_WIDTH = 1024
A_GROUP_DIM = A_WIDTH // A_GROUPS
B_HEADS = 8
B_HEAD_DIM = 128
B_WIDTH = B_HEADS * B_HEAD_DIM
Q_BLOCK = 128
N_EXPERTS = 32
TOP_K = 4
D_EXPERT = 512
SWIGLU_LIMIT = 7.0
SWIGLU_ALPHA = 1.702
DEEPNORM_ALPHA = (2.0 * DEPTH) ** 0.25
DEEPNORM_BETA = (8.0 * DEPTH) ** -0.25
LN_EPS = 1e-5
NEG_INF = -1e30
IN_SPLITS = (A_WIDTH, 2 * A_WIDTH, 2 * A_WIDTH + B_WIDTH, 2 * A_WIDTH + 2 * B_WIDTH,
             2 * A_WIDTH + 3 * B_WIDTH, 2 * A_WIDTH + 3 * B_WIDTH + B_HEADS,
             2 * A_WIDTH + 3 * B_WIDTH + B_HEADS + D_MODEL)
IN_COLS = 2 * A_WIDTH + 3 * B_WIDTH + B_HEADS + 2 * D_MODEL

kernel_name = "hybrid_sgu_fox_moe_deepnorm_adaln"


def layer_norm(x, g, b):
    xf = x.astype(jnp.float32)
    mu = jnp.mean(xf, axis=-1, keepdims=True)
    var = jnp.mean(jnp.square(xf - mu), axis=-1, keepdims=True)
    return ((xf - mu) * lax.rsqrt(var + LN_EPS) * g + b).astype(x.dtype)


def chunked_sgu(u, v, ln_g, w_s, b_s):
    bsz, seq, _ = v.shape
    n_chunks = seq // CHUNK
    vf = v.astype(jnp.float32).reshape(bsz, n_chunks, CHUNK, A_GROUPS, A_GROUP_DIM)
    mu = jnp.mean(vf, axis=-1, keepdims=True)
    var = jnp.mean(jnp.square(vf - mu), axis=-1, keepdims=True)
    vn = (vf - mu) * lax.rsqrt(var + LN_EPS) * ln_g
    causal = jnp.tril(jnp.ones((CHUNK, CHUNK), dtype=bool))
    w = jnp.where(causal, w_s.astype(jnp.float32), 0.0)
    mixed = jnp.einsum('gts,bnsgc->bntgc', w, vn) + b_s.astype(jnp.float32).T[None, None, :, :, None]
    return u * mixed.reshape(bsz, seq, A_WIDTH).astype(u.dtype)


def forgetting_attention(q, k, v, f_logit, b_f):
    bsz, seq, _ = q.shape
    n_blocks = seq // Q_BLOCK
    qf = q.astype(jnp.float32).reshape(bsz, seq, B_HEADS, B_HEAD_DIM)
    kf = k.astype(jnp.float32).reshape(bsz, seq, B_HEADS, B_HEAD_DIM)
    vf = v.astype(jnp.float32).reshape(bsz, seq, B_HEADS, B_HEAD_DIM)
    log_f = jax.nn.log_sigmoid(f_logit.astype(jnp.float32) + b_f.astype(jnp.float32))
    cum = jnp.transpose(jnp.cumsum(log_f, axis=1), (0, 2, 1))
    q_blocks = qf.reshape(bsz, n_blocks, Q_BLOCK, B_HEADS, B_HEAD_DIM).transpose(1, 0, 2, 3, 4)
    cq_blocks = cum.reshape(bsz, B_HEADS, n_blocks, Q_BLOCK).transpose(2, 0, 1, 3)
    q_pos = jnp.arange(seq, dtype=jnp.int32).reshape(n_blocks, Q_BLOCK)
    k_pos = jnp.arange(seq, dtype=jnp.int32)
    scale = B_HEAD_DIM ** -0.5

    def one_block(args):
        q_i, cq_i, pos_i = args
        s = jnp.einsum('bqhd,bkhd->bhqk', q_i, kf) * scale
        s = s + cq_i[..., None] - cum[:, :, None, :]
        s = jnp.where(pos_i[:, None] >= k_pos[None, :], s, NEG_INF)
        p = jax.nn.softmax(s, axis=-1)
        return jnp.einsum('bhqk,bkhd->bqhd', p, vf)

    o = lax.map(one_block, (q_blocks, cq_blocks, q_pos))
    return o.transpose(1, 0, 2, 3, 4).reshape(bsz, seq, B_WIDTH).astype(q.dtype)


def hybrid_mixer(h, w_in, b_f, sgu_ln_g, spatial_w, spatial_b, w_proj_a, w_proj_b, w_o):
    z = h @ w_in
    u_a, v_a, q_b, k_b, v_b, f_b, g_a, g_b = jnp.split(z, IN_SPLITS, axis=-1)
    y_a = chunked_sgu(jax.nn.gelu(u_a), jax.nn.gelu(v_a), sgu_ln_g, spatial_w, spatial_b)
    y_b = forgetting_attention(q_b, k_b, v_b, f_b, b_f)
    merged = jax.nn.sigmoid(g_a) * (y_a @ w_proj_a) + jax.nn.sigmoid(g_b) * (y_b @ w_proj_b)
    return merged @ w_o


def moe_ffn(h, router_w, router_b, w_gate, b_gate, w_up, b_up, w_down, b_down):
    logits = (h @ router_w + router_b).astype(jnp.float32)
    top_v, top_i = lax.top_k(logits, TOP_K)
    top_w = jax.nn.softmax(top_v, axis=-1)
    combine = jnp.sum(jax.nn.one_hot(top_i, N_EXPERTS, dtype=jnp.float32) * top_w[..., None], axis=-2)
    out = jnp.zeros(h.shape, jnp.float32)
    for e in range(N_EXPERTS):
        g = jnp.minimum(h @ w_gate[e] + b_gate[e], SWIGLU_LIMIT)
        u = jnp.clip(h @ w_up[e] + b_up[e], -SWIGLU_LIMIT, SWIGLU_LIMIT)
        act = g * jax.nn.sigmoid(SWIGLU_ALPHA * g) * (u + 1.0)
        out = out + combine[..., e:e + 1] * (act @ w_down[e] + b_down[e]).astype(jnp.float32)
    return out.astype(h.dtype)


def setup_inputs(seed: int = 0) -> dict:
    key = jax.random.key(seed)
    ks = jax.random.split(key, 24)
    L, D, E, F = DEPTH, D_MODEL, N_EXPERTS, D_EXPERT

    def nrm(k, shape, scale):
        return jax.random.normal(k, shape, jnp.float32) * scale

    tril = jnp.tril(jnp.ones((CHUNK, CHUNK), jnp.float32))
    return {
        "x": nrm(ks[0], (BATCH, SEQ, D), 1.0),
        "c": nrm(ks[1], (BATCH, D), 1.0),
        "w_ada": nrm(ks[2], (L, D, 6 * D), 0.1 * D ** -0.5),
        "b_ada": nrm(ks[3], (L, 6 * D), 0.02),
        "w_in": nrm(ks[4], (L, D, IN_COLS), D ** -0.5),
        "b_f": 2.0 + nrm(ks[5], (L, B_HEADS), 0.5),
        "sgu_ln_g": 1.0 + nrm(ks[6], (L, A_GROUPS, A_GROUP_DIM), 0.02),
        "spatial_w": tril * nrm(ks[7], (L, A_GROUPS, CHUNK, CHUNK), CHUNK ** -0.5),
        "spatial_b": 1.0 + nrm(ks[8], (L, A_GROUPS, CHUNK), 0.02),
        "w_proj_a": nrm(ks[9], (L, A_WIDTH, D), A_WIDTH ** -0.5),
        "w_proj_b": nrm(ks[10], (L, B_WIDTH, D), B_WIDTH ** -0.5),
        "w_o": nrm(ks[11], (L, D, D), DEEPNORM_BETA * D ** -0.5),
        "ln1_g": 1.0 + nrm(ks[12], (L, D), 0.02),
        "ln1_b": nrm(ks[13], (L, D), 0.02),
        "router_w": nrm(ks[14], (L, D, E), D ** -0.5),
        "router_b": nrm(ks[15], (L, E), 0.01),
        "w_gate": nrm(ks[16], (L, E, D, F), D ** -0.5),
        "b_gate": nrm(ks[17], (L, E, F), 0.02),
        "w_up": nrm(ks[18], (L, E, D, F), D ** -0.5),
        "b_up": nrm(ks[19], (L, E, F), 0.02),
        "w_down": nrm(ks[20], (L, E, F, D), DEEPNORM_BETA * F ** -0.5),
        "b_down": nrm(ks[21], (L, E, D), 0.02),
        "ln2_g": 1.0 + nrm(ks[22], (L, D), 0.02),
        "ln2_b": nrm(ks[23], (L, D), 0.02),
    }


def reference(x, c, w_ada, b_ada, w_in, b_f, sgu_ln_g, spatial_w, spatial_b, w_proj_a, w_proj_b,
              w_o, ln1_g, ln1_b, router_w, router_b, w_gate, b_gate, w_up, b_up, w_down, b_down,
              ln2_g, ln2_b):
    cond = jax.nn.silu(c)
    for l in range(DEPTH):
        mod = (cond @ w_ada[l] + b_ada[l])[:, None, :]
        sh1, sc1, g1, sh2, sc2, g2 = jnp.split(mod, 6, axis=-1)
        h = x * (1.0 + sc1) + sh1
        mix = hybrid_mixer(h, w_in[l], b_f[l], sgu_ln_g[l], spatial_w[l], spatial_b[l],
                           w_proj_a[l], w_proj_b[l], w_o[l])
        x = layer_norm(DEEPNORM_ALPHA * x + (1.0 + g1) * mix, ln1_g[l], ln1_b[l])
        h = x * (1.0 + sc2) + sh2
        ffn = moe_ffn(h, router_w[l], router_b[l], w_gate[l], b_gate[l], w_up[l], b_up[l],
                      w_down[l], b_down[l])
        x = layer_norm(DEEPNORM_ALPHA * x + (1.0 + g2) * ffn, ln2_g[l], ln2_b[l])
    return x
```

```python
import functools

import jax
import jax.numpy as jnp
from jax import lax
from jax.experimental import pallas as pl
from jax.experimental.pallas import tpu as pltpu

F32 = jnp.float32
BF16 = jnp.bfloat16
HIGHEST = lax.Precision.HIGHEST

D_MODEL = 2048
DEPTH = 4
CHUNK = 128
A_GROUPS = 8
A_WIDTH = 1024
B_HEADS = 8
B_HEAD_DIM = 128
B_WIDTH = 1024
N_EXPERTS = 32
TOP_K = 4
D_EXPERT = 512
SWIGLU_LIMIT = 7.0
SWIGLU_ALPHA = 1.702
DEEPNORM_ALPHA = (2.0 * DEPTH) ** 0.25
LN_EPS = 1e-5
MASK_VALUE = -1e30

LANES = 128
VMEM_LIMIT = 56 * 1024 * 1024

Z_COLS = 2 * D_MODEL + 2 * A_WIDTH + 3 * B_WIDTH
Z_GA, Z_GB, Z_U, Z_V, Z_Q, Z_K, Z_VB = 0, 2048, 4096, 5120, 6144, 7168, 8192

TN_IN = 1024
TM_IN = 1024
TM_SGU = 256
TQ = 256
TK = 256
TM_POST = 256
TM_MOE = 256
TM_COMB = 128
CUM_CHUNK = 512
ROUTER_PAD = 128


def _cparams(sem, vmem=VMEM_LIMIT):
    return pltpu.CompilerParams(dimension_semantics=sem, vmem_limit_bytes=vmem)


def _adaln_body(c_ref, w_ref, b_ref, o_ref):
    c = c_ref[...]
    cond = c * jax.nn.sigmoid(c)
    o_ref[0] = jnp.sum(w_ref[0] * cond, axis=0, keepdims=True) + b_ref[0]


def _adaln(c_col, w_ada, b_ada):
    depth, d, n = w_ada.shape
    tn = 1024
    return pl.pallas_call(
        _adaln_body,
        grid=(depth, n // tn),
        in_specs=[
            pl.BlockSpec((d, 1), lambda l, j: (0, 0)),
            pl.BlockSpec((1, d, tn), lambda l, j: (l, 0, j)),
            pl.BlockSpec((1, 1, tn), lambda l, j: (l, 0, j)),
        ],
        out_specs=pl.BlockSpec((1, 1, tn), lambda l, j: (l, 0, j)),
        out_shape=jax.ShapeDtypeStruct((depth, 1, n), F32),
        compiler_params=_cparams(("arbitrary", "arbitrary")),
        name="adaln",
    )(c_col, w_ada, b_ada.reshape(depth, 1, n))


def _gelu_tanh(x):
    return 0.5 * x * (1.0 + jnp.tanh(0.7978845608028654 * (x + 0.044715 * (x * x * x))))


def _inproj_body(x_ref, sc_ref, sh_ref, w_ref, wf_ref, z_ref, f_ref, h_scr):
    j = pl.program_id(1)

    @pl.when(j == 0)
    def _():
        h = x_ref[...] * (1.0 + sc_ref[...]) + sh_ref[...]
        h_scr[...] = h.astype(BF16)
        f_ref[...] = jnp.dot(h, wf_ref[...], precision=HIGHEST, preferred_element_type=F32)

    acc = jnp.dot(h_scr[...], w_ref[...], preferred_element_type=F32)
    n_sig = (2 * D_MODEL) // TN_IN
    n_gelu = (2 * A_WIDTH) // TN_IN

    @pl.when(j < n_sig)
    def _():
        z_ref[...] = jax.nn.sigmoid(acc).astype(BF16)

    @pl.when(jnp.logical_and(j >= n_sig, j < n_sig + n_gelu))
    def _():
        z_ref[...] = _gelu_tanh(acc).astype(BF16)

    @pl.when(j >= n_sig + n_gelu)
    def _():
        z_ref[...] = acc.astype(BF16)


def _inproj(x, sc, sh, w_main, w_f):
    s, d = x.shape
    tm = min(TM_IN, s)
    return pl.pallas_call(
        _inproj_body,
        grid=(s // tm, Z_COLS // TN_IN),
        in_specs=[
            pl.BlockSpec((tm, d), lambda i, j: (i, 0)),
            pl.BlockSpec((1, d), lambda i, j: (0, 0)),
            pl.BlockSpec((1, d), lambda i, j: (0, 0)),
            pl.BlockSpec((d, TN_IN), lambda i, j: (0, j)),
            pl.BlockSpec((d, B_HEADS), lambda i, j: (0, 0)),
        ],
        out_specs=[
            pl.BlockSpec((tm, TN_IN), lambda i, j: (i, j)),
            pl.BlockSpec((tm, B_HEADS), lambda i, j: (i, 0)),
        ],
        out_shape=[
            jax.ShapeDtypeStruct((s, Z_COLS), BF16),
            jax.ShapeDtypeStruct((s, B_HEADS), F32),
        ],
        scratch_shapes=[pltpu.VMEM((tm, d), BF16)],
        compiler_params=_cparams(("arbitrary", "arbitrary")),
        name="inproj",
    )(x, sc, sh, w_main, w_f)


def _cum_body(f_ref, bf_ref, o_ref):
    n_chunks = f_ref.shape[1] // CUM_CHUNK
    row = lax.broadcasted_iota(jnp.int32, (CUM_CHUNK, CUM_CHUNK), 0)
    col = lax.broadcasted_iota(jnp.int32, (CUM_CHUNK, CUM_CHUNK), 1)
    upper = jnp.where(row <= col, 1.0, 0.0).astype(F32)

    def body(i, carry):
        sl = pl.ds(pl.multiple_of(i * CUM_CHUNK, CUM_CHUNK), CUM_CHUNK)
        xf = f_ref[:, sl] + bf_ref[...]
        logf = jnp.minimum(xf, 0.0) - jnp.log1p(jnp.exp(-jnp.abs(xf)))
        cs = jnp.dot(logf, upper, precision=HIGHEST, preferred_element_type=F32) + carry
        o_ref[:, sl] = cs
        return cs[:, CUM_CHUNK - 1:CUM_CHUNK]

    lax.fori_loop(0, n_chunks, body, jnp.zeros((B_HEADS, 1), F32))


def _forget_cumsum(f_t, b_f_col):
    h, s = f_t.shape
    return pl.pallas_call(
        _cum_body,
        out_shape=jax.ShapeDtypeStruct((h, s), F32),
        compiler_params=pltpu.CompilerParams(vmem_limit_bytes=VMEM_LIMIT),
        name="forget_cumsum",
    )(f_t, b_f_col)


def _sgu_body(u_ref, v_ref, g_ref, w_ref, b_ref, o_ref):
    tm = u_ref.shape[0]
    row = lax.broadcasted_iota(jnp.int32, (CHUNK, CHUNK), 0)
    col = lax.broadcasted_iota(jnp.int32, (CHUNK, CHUNK), 1)
    causal = row >= col
    for g in range(A_GROUPS):
        cols = slice(g * CHUNK, (g + 1) * CHUNK)
        w = jnp.where(causal, w_ref[g], 0.0).astype(BF16)
        bias = b_ref[:, g:g + 1]
        gain = g_ref[:, cols]
        for ch in range(tm // CHUNK):
            rows = slice(ch * CHUNK, (ch + 1) * CHUNK)
            v = v_ref[rows, cols].astype(F32)
            mu = jnp.mean(v, axis=-1, keepdims=True)
            vc = v - mu
            var = jnp.mean(vc * vc, axis=-1, keepdims=True)
            vn = vc * lax.rsqrt(var + LN_EPS) * gain
            mixed = jnp.dot(w, vn.astype(BF16), preferred_element_type=F32) + bias
            o_ref[rows, cols] = (u_ref[rows, cols].astype(F32) * mixed).astype(BF16)


def _sgu(z, ln_g, w_s, b_s_t):
    s = z.shape[0]
    tm = min(TM_SGU, s)
    return pl.pallas_call(
        _sgu_body,
        grid=(s // tm,),
        in_specs=[
            pl.BlockSpec((tm, A_WIDTH), lambda i: (i, Z_U // A_WIDTH)),
            pl.BlockSpec((tm, A_WIDTH), lambda i: (i, Z_V // A_WIDTH)),
            pl.BlockSpec((1, A_WIDTH), lambda i: (0, 0)),
            pl.BlockSpec((A_GROUPS, CHUNK, CHUNK), lambda i: (0, 0, 0)),
            pl.BlockSpec((CHUNK, A_GROUPS), lambda i: (0, 0)),
        ],
        out_specs=pl.BlockSpec((tm, A_WIDTH), lambda i: (i, 0)),
        out_shape=jax.ShapeDtypeStruct((s, A_WIDTH), BF16),
        compiler_params=_cparams(("arbitrary",)),
        name="sgu",
    )(z, z, ln_g, w_s, b_s_t)


def _attn_body(lo_ref, q_ref, k_ref, v_ref, cq_ref, ck_ref, o_ref, m_scr, l_scr, acc_scr):
    h = pl.program_id(0)
    i = pl.program_id(1)
    tq = q_ref.shape[0]
    scale = B_HEAD_DIM ** -0.5
    q = q_ref[...]
    lane8 = lax.broadcasted_iota(jnp.int32, (tq, B_HEADS), 1)
    cq = jnp.sum(jnp.where(lane8 == h, cq_ref[...], 0.0), axis=1, keepdims=True)

    m_scr[...] = jnp.full((tq, 1), MASK_VALUE, F32)
    l_scr[...] = jnp.zeros((tq, 1), F32)
    acc_scr[...] = jnp.zeros((tq, B_HEAD_DIM), F32)

    def step(j, masked):
        ks = pl.ds(pl.multiple_of(j * TK, TK), TK)
        k = k_ref[ks, :]
        v = v_ref[ks, :]
        s = lax.dot_general(q, k, (((1,), (1,)), ((), ())), preferred_element_type=F32)
        s = s * scale + (cq - ck_ref[0, :, ks])
        if masked:
            r = lax.broadcasted_iota(jnp.int32, (tq, TK), 0)
            c = lax.broadcasted_iota(jnp.int32, (tq, TK), 1)
            s = jnp.where(r >= c, s, MASK_VALUE)
        m_prev = m_scr[...]
        m_new = jnp.maximum(m_prev, jnp.max(s, axis=1, keepdims=True))
        p = jnp.exp(s - m_new)
        alpha = jnp.exp(m_prev - m_new)
        l_scr[...] = alpha * l_scr[...] + jnp.sum(p, axis=1, keepdims=True)
        acc_scr[...] = alpha * acc_scr[...] + jnp.dot(p.astype(BF16), v, preferred_element_type=F32)
        m_scr[...] = m_new

    def body(j, carry):
        step(j, False)
        return carry

    lax.fori_loop(lo_ref[h, i], i, body, 0)
    step(i, True)
    o_ref[...] = (acc_scr[...] / l_scr[...]).astype(BF16)


def _attention(z, cum_sh, cum_hs, lo):
    s = z.shape[0]
    assert TQ == TK
    tq = TQ
    grid_spec = pltpu.PrefetchScalarGridSpec(
        num_scalar_prefetch=1,
        grid=(B_HEADS, s // tq),
        in_specs=[
            pl.BlockSpec((tq, B_HEAD_DIM), lambda h, i, lo: (i, Z_Q // B_HEAD_DIM + h)),
            pl.BlockSpec((s, B_HEAD_DIM), lambda h, i, lo: (0, Z_K // B_HEAD_DIM + h)),
            pl.BlockSpec((s, B_HEAD_DIM), lambda h, i, lo: (0, Z_VB // B_HEAD_DIM + h)),
            pl.BlockSpec((tq, B_HEADS), lambda h, i, lo: (i, 0)),
            pl.BlockSpec((1, 1, s), lambda h, i, lo: (h, 0, 0)),
        ],
        out_specs=pl.BlockSpec((tq, B_HEAD_DIM), lambda h, i, lo: (i, h)),
        scratch_shapes=[
            pltpu.VMEM((tq, 1), F32),
            pltpu.VMEM((tq, 1), F32),
            pltpu.VMEM((tq, B_HEAD_DIM), F32),
        ],
    )
    return pl.pallas_call(
        _attn_body,
        grid_spec=grid_spec,
        out_shape=jax.ShapeDtypeStruct((s, B_WIDTH), BF16),
        compiler_params=_cparams(("arbitrary", "arbitrary")),
        name="fox_attention",
    )(lo, z, z, z, cum_sh, cum_hs.reshape(B_HEADS, 1, s))


def _layer_norm(y, g, b):
    mu = jnp.mean(y, axis=-1, keepdims=True)
    yc = y - mu
    var = jnp.mean(yc * yc, axis=-1, keepdims=True)
    return yc * lax.rsqrt(var + LN_EPS) * g + b


def _post_body(ya_ref, yb_ref, ga_ref, gb_ref, x_ref, wa_ref, wb_ref, wo_ref, g1_ref, lng_ref,
               lnb_ref, sc_ref, sh_ref, rw_ref, rb_ref, x1_ref, h2_ref, ti_ref, tw_ref):
    tm = x_ref.shape[0]
    a = jnp.dot(ya_ref[...], wa_ref[...], preferred_element_type=F32)
    b = jnp.dot(yb_ref[...], wb_ref[...], preferred_element_type=F32)
    merged = ga_ref[...].astype(F32) * a + gb_ref[...].astype(F32) * b
    mix = jnp.dot(merged.astype(BF16), wo_ref[...], preferred_element_type=F32)
    x1 = _layer_norm(DEEPNORM_ALPHA * x_ref[...] + (1.0 + g1_ref[...]) * mix,
                     lng_ref[...], lnb_ref[...])
    x1_ref[...] = x1
    h2 = x1 * (1.0 + sc_ref[...]) + sh_ref[...]
    h2_ref[...] = h2
    logits = jnp.dot(h2, rw_ref[...], precision=HIGHEST, preferred_element_type=F32) + rb_ref[...]
    lane = lax.broadcasted_iota(jnp.int32, (tm, ROUTER_PAD), 1)
    lane_f = lane.astype(F32)
    neg_inf = float("-inf")
    work = logits
    vals, idxs = [], []
    for _ in range(TOP_K):
        m = jnp.max(work, axis=1, keepdims=True)
        idx = jnp.min(jnp.where(work == m, lane_f, float(ROUTER_PAD)), axis=1, keepdims=True)
        vals.append(m)
        idxs.append(idx)
        work = jnp.where(lane_f == idx, neg_inf, work)
    exps = [jnp.exp(v - vals[0]) for v in vals]
    denom = exps[0] + exps[1] + exps[2] + exps[3]
    ti = jnp.zeros((tm, ROUTER_PAD), F32)
    tw = jnp.zeros((tm, ROUTER_PAD), F32)
    for k in range(TOP_K):
        ti = jnp.where(lane == k, idxs[k], ti)
        tw = jnp.where(lane == k, exps[k] / denom, tw)
    ti_ref[...] = ti.astype(jnp.int32)
    tw_ref[...] = tw


def _post_mixer(ya, yb, z, x, wa, wb, wo, g1, ln_g, ln_b, sc2, sh2, rw_pad, rb_pad):
    s, d = x.shape
    tm = min(TM_POST, s)
    row = lambda i: (i, 0)
    const = lambda i: (0, 0)
    single = pl.Buffered(1)
    return pl.pallas_call(
        _post_body,
        grid=(s // tm,),
        in_specs=[
            pl.BlockSpec((tm, A_WIDTH), row),
            pl.BlockSpec((tm, B_WIDTH), row),
            pl.BlockSpec((tm, d), lambda i: (i, Z_GA // D_MODEL)),
            pl.BlockSpec((tm, d), lambda i: (i, Z_GB // D_MODEL)),
            pl.BlockSpec((tm, d), row),
            pl.BlockSpec((A_WIDTH, d), const, pipeline_mode=single),
            pl.BlockSpec((B_WIDTH, d), const, pipeline_mode=single),
            pl.BlockSpec((d, d), const, pipeline_mode=single),
            pl.BlockSpec((1, d), const),
            pl.BlockSpec((1, d), const),
            pl.BlockSpec((1, d), const),
            pl.BlockSpec((1, d), const),
            pl.BlockSpec((1, d), const),
            pl.BlockSpec((d, ROUTER_PAD), const),
            pl.BlockSpec((1, ROUTER_PAD), const),
        ],
        out_specs=[
            pl.BlockSpec((tm, d), row),
            pl.BlockSpec((tm, d), row),
            pl.BlockSpec((tm, ROUTER_PAD), row),
            pl.BlockSpec((tm, ROUTER_PAD), row),
        ],
        out_shape=[
            jax.ShapeDtypeStruct((s, d), F32),
            jax.ShapeDtypeStruct((s, d), F32),
            jax.ShapeDtypeStruct((s, ROUTER_PAD), jnp.int32),
            jax.ShapeDtypeStruct((s, ROUTER_PAD), F32),
        ],
        compiler_params=_cparams(("arbitrary",)),
        name="post_mixer",
    )(ya, yb, z, z, x, wa, wb, wo, g1, ln_g, ln_b, sc2, sh2, rw_pad, rb_pad)


def _row_copy(src_hbm, dst_vmem, src_row, dst_row, sem):
    return pltpu.make_async_copy(src_hbm.at[pl.ds(src_row, 1)], dst_vmem.at[pl.ds(dst_row, 1)], sem)


def _moe_body(te_ref, nv_ref, rows_ref, h_hbm, wg_ref, bg_ref, wu_ref, bu_ref, wd_ref, bd_ref,
              y_ref, xbuf, sem):
    i = pl.program_id(0)
    tm = xbuf.shape[0]

    @pl.when(i < nv_ref[0])
    def _():
        def issue(r, carry):
            _row_copy(h_hbm, xbuf, rows_ref[0, 0, r], r, sem).start()
            return carry

        lax.fori_loop(0, tm, issue, 0, unroll=8)

        def drain(r, carry):
            _row_copy(h_hbm, xbuf, 0, r, sem).wait()
            return carry

        lax.fori_loop(0, tm, drain, 0, unroll=8)

        xb = xbuf[...].astype(BF16)
        g = jnp.minimum(jnp.dot(xb, wg_ref[0], preferred_element_type=F32) + bg_ref[0], SWIGLU_LIMIT)
        u = jnp.clip(jnp.dot(xb, wu_ref[0], preferred_element_type=F32) + bu_ref[0],
                     -SWIGLU_LIMIT, SWIGLU_LIMIT)
        act = g * jax.nn.sigmoid(SWIGLU_ALPHA * g) * (u + 1.0)
        y_ref[...] = jnp.dot(act.astype(BF16), wd_ref[0], preferred_element_type=F32) + bd_ref[0]

    @pl.when(i >= nv_ref[0])
    def _():
        y_ref[...] = jnp.zeros(y_ref.shape, F32)


def _moe_experts(tile_expert, n_valid, row_token, h2, wg, bg, wu, bu, wd, bd):
    s, d = h2.shape
    n_tiles = row_token.shape[0]
    tm = row_token.shape[2]
    wmap = lambda i, te, nv: (te[i], 0, 0)
    grid_spec = pltpu.PrefetchScalarGridSpec(
        num_scalar_prefetch=2,
        grid=(n_tiles,),
        in_specs=[
            pl.BlockSpec((1, 1, tm), lambda i, te, nv: (i, 0, 0), memory_space=pltpu.SMEM),
            pl.BlockSpec(memory_space=pl.ANY),
            pl.BlockSpec((1, d, D_EXPERT), wmap),
            pl.BlockSpec((1, 1, D_EXPERT), wmap),
            pl.BlockSpec((1, d, D_EXPERT), wmap),
            pl.BlockSpec((1, 1, D_EXPERT), wmap),
            pl.BlockSpec((1, D_EXPERT, d), wmap),
            pl.BlockSpec((1, 1, d), wmap),
        ],
        out_specs=pl.BlockSpec((tm, d), lambda i, te, nv: (i, 0)),
        scratch_shapes=[pltpu.VMEM((tm, d), F32), pltpu.SemaphoreType.DMA(())],
    )
    return pl.pallas_call(
        _moe_body,
        grid_spec=grid_spec,
        out_shape=jax.ShapeDtypeStruct((n_tiles * tm, d), F32),
        compiler_params=_cparams(("arbitrary",)),
        name="moe_experts",
    )(tile_expert, n_valid, row_token, h2, wg, bg, wu, bu, wd, bd)


def _combine_body(pos_ref, tw_ref, x_ref, g2_ref, lng_ref, lnb_ref, y_hbm, o_ref, ybuf, sem):
    tm = x_ref.shape[0]

    def issue(t, carry):
        for k in range(TOP_K):
            _row_copy(y_hbm, ybuf.at[k], pos_ref[0, 0, t * TOP_K + k], t, sem).start()
        return carry

    lax.fori_loop(0, tm, issue, 0, unroll=4)

    def drain(t, carry):
        for k in range(TOP_K):
            _row_copy(y_hbm, ybuf.at[k], 0, t, sem).wait()
        return carry

    lax.fori_loop(0, tm, drain, 0, unroll=4)

    tw = tw_ref[...]
    ffn = tw[:, 0:1] * ybuf[0]
    for k in range(1, TOP_K):
        ffn = ffn + tw[:, k:k + 1] * ybuf[k]
    o_ref[...] = _layer_norm(DEEPNORM_ALPHA * x_ref[...] + (1.0 + g2_ref[...]) * ffn,
                             lng_ref[...], lnb_ref[...])


def _combine(pos, tw, x1, g2, ln_g, ln_b, y_sorted):
    s, d = x1.shape
    tm = min(TM_COMB, s)
    row = lambda i: (i, 0)
    const = lambda i: (0, 0)
    return pl.pallas_call(
        _combine_body,
        grid=(s // tm,),
        in_specs=[
            pl.BlockSpec((1, 1, tm * TOP_K), lambda i: (i, 0, 0), memory_space=pltpu.SMEM),
            pl.BlockSpec((tm, ROUTER_PAD), row),
            pl.BlockSpec((tm, d), row),
            pl.BlockSpec((1, d), const),
            pl.BlockSpec((1, d), const),
            pl.BlockSpec((1, d), const),
            pl.BlockSpec(memory_space=pl.ANY),
        ],
        out_specs=pl.BlockSpec((tm, d), row),
        out_shape=jax.ShapeDtypeStruct((s, d), F32),
        scratch_shapes=[pltpu.VMEM((TOP_K, tm, d), F32), pltpu.SemaphoreType.DMA(())],
        compiler_params=_cparams(("arbitrary",)),
        name="moe_combine",
    )(pos.reshape(s // tm, 1, tm * TOP_K), tw, x1, g2, ln_g, ln_b, y_sorted)


def _routing_tables(top_i, s):
    tm = TM_MOE
    n_pairs = s * TOP_K
    n_tiles = n_pairs // tm + N_EXPERTS
    flat_e = top_i.reshape(n_pairs)
    onehot = (flat_e[:, None] == jnp.arange(N_EXPERTS, dtype=jnp.int32)[None, :]).astype(jnp.int32)
    csum = jnp.cumsum(onehot, axis=0)
    rank = jnp.sum(csum * onehot, axis=1) - 1
    counts = csum[-1]
    padded = ((counts + tm - 1) // tm) * tm
    ends = jnp.cumsum(padded)
    starts = ends - padded
    pos = jnp.sum(onehot * starts[None, :], axis=1) + rank
    row_token = jnp.zeros((n_tiles * tm,), jnp.int32).at[pos].set(
        jnp.arange(n_pairs, dtype=jnp.int32) // TOP_K)
    tile_start = jnp.arange(n_tiles, dtype=jnp.int32) * tm
    tile_expert = jnp.sum((tile_start[:, None] >= ends[None, :]).astype(jnp.int32), axis=1)
    tile_expert = jnp.minimum(tile_expert, N_EXPERTS - 1).astype(jnp.int32)
    n_valid = (ends[-1] // tm).astype(jnp.int32).reshape(1)
    return pos.astype(jnp.int32), row_token.reshape(n_tiles, 1, tm), tile_expert, n_valid


def kernel(x, c, w_ada, b_ada, w_in, b_f, sgu_ln_g, spatial_w, spatial_b, w_proj_a, w_proj_b, w_o,
           ln1_g, ln1_b, router_w, router_b, w_gate, b_gate, w_up, b_up, w_down, b_down, ln2_g, ln2_b):
    bsz, s, d = x.shape
    assert bsz == 1
    depth = w_ada.shape[0]
    xs = x.reshape(s, d)
    mod = _adaln(c.reshape(d, 1), w_ada, b_ada)

    n_a = 2 * A_WIDTH
    n_b = 3 * B_WIDTH
    w_main = jnp.concatenate(
        [w_in[:, :, n_a + n_b + B_HEADS:], w_in[:, :, :n_a + n_b]], axis=-1).astype(BF16)
    w_f = w_in[:, :, n_a + n_b:n_a + n_b + B_HEADS]
    wa16 = w_proj_a.astype(BF16)
    wb16 = w_proj_b.astype(BF16)
    wo16 = w_o.astype(BF16)
    wg16 = w_gate.astype(BF16)
    wu16 = w_up.astype(BF16)
    wd16 = w_down.astype(BF16)
    rw_pad = jnp.pad(router_w, ((0, 0), (0, 0), (0, ROUTER_PAD - N_EXPERTS)))
    rb_pad = jnp.pad(router_b, ((0, 0), (0, ROUTER_PAD - N_EXPERTS)), constant_values=float("-inf"))
    lo = jnp.zeros((B_HEADS, s // TQ), jnp.int32)

    for l in range(depth):
        sh1, sc1, g1, sh2, sc2, g2 = [mod[l, :, k * d:(k + 1) * d] for k in range(6)]
        z, f_logit = _inproj(xs, sc1, sh1, w_main[l], w_f[l])
        cum_hs = _forget_cumsum(f_logit.T, b_f[l].reshape(B_HEADS, 1))
        ya = _sgu(z, sgu_ln_g[l].reshape(1, A_WIDTH), spatial_w[l], spatial_b[l].T)
        yb = _attention(z, cum_hs.T, cum_hs, lo)
        x1, h2, top_i, top_w = _post_mixer(
            ya, yb, z, xs, wa16[l], wb16[l], wo16[l], g1, ln1_g[l].reshape(1, d),
            ln1_b[l].reshape(1, d), sc2, sh2, rw_pad[l], rb_pad[l].reshape(1, ROUTER_PAD))
        pos, row_token, tile_expert, n_valid = _routing_tables(top_i[:, :TOP_K], s)
        y_sorted = _moe_experts(
            tile_expert, n_valid, row_token, h2, wg16[l], b_gate[l].reshape(N_EXPERTS, 1, D_EXPERT),
            wu16[l], b_up[l].reshape(N_EXPERTS, 1, D_EXPERT), wd16[l],
            b_down[l].reshape(N_EXPERTS, 1, d))
        xs = _combine(pos, top_w, x1, g2, ln2_g[l].reshape(1, d), ln2_b[l].reshape(1, d), y_sorted)
    return xs.reshape(bsz, s, d)
```

```python
import functools

import jax
import jax.numpy as jnp
from jax import lax
from jax.experimental import pallas as pl
from jax.experimental.pallas import tpu as pltpu

F32 = jnp.float32
BF16 = jnp.bfloat16
HIGHEST = lax.Precision.HIGHEST

D_MODEL = 2048
DEPTH = 4
CHUNK = 128
A_GROUPS = 8
A_WIDTH = 1024
B_HEADS = 8
B_HEAD_DIM = 128
B_WIDTH = 1024
N_EXPERTS = 32
TOP_K = 4
D_EXPERT = 512
SWIGLU_LIMIT = 7.0
SWIGLU_ALPHA = 1.702
DEEPNORM_ALPHA = (2.0 * DEPTH) ** 0.25
LN_EPS = 1e-5
MASK_VALUE = -1e30

LANES = 128
VMEM_LIMIT = 56 * 1024 * 1024

Z_COLS = 2 * D_MODEL + 2 * A_WIDTH + 3 * B_WIDTH
Z_GA, Z_GB, Z_U, Z_V, Z_Q, Z_K, Z_VB = 0, 2048, 4096, 5120, 6144, 7168, 8192

TN_IN = 1024
TM_IN = 1024
TM_SGU = 256
TQ = 512
PRUNE_MARGIN = 110.0
TM_POST = 256
TM_MOE = 256
TM_COMB = 128
CUM_CHUNK = 512
ROUTER_PAD = 128


def _cparams(sem, vmem=VMEM_LIMIT):
    return pltpu.CompilerParams(dimension_semantics=sem, vmem_limit_bytes=vmem)


def _adaln_body(c_ref, w_ref, b_ref, o_ref):
    c = c_ref[...]
    cond = c * jax.nn.sigmoid(c)
    o_ref[0] = jnp.sum(w_ref[0] * cond, axis=0, keepdims=True) + b_ref[0]


def _adaln(c_col, w_ada, b_ada):
    depth, d, n = w_ada.shape
    tn = 1024
    return pl.pallas_call(
        _adaln_body,
        grid=(depth, n // tn),
        in_specs=[
            pl.BlockSpec((d, 1), lambda l, j: (0, 0)),
            pl.BlockSpec((1, d, tn), lambda l, j: (l, 0, j)),
            pl.BlockSpec((1, 1, tn), lambda l, j: (l, 0, j)),
        ],
        out_specs=pl.BlockSpec((1, 1, tn), lambda l, j: (l, 0, j)),
        out_shape=jax.ShapeDtypeStruct((depth, 1, n), F32),
        compiler_params=_cparams(("arbitrary", "arbitrary")),
        name="adaln",
    )(c_col, w_ada, b_ada.reshape(depth, 1, n))


def _gelu_tanh(x):
    return 0.5 * x * (1.0 + jnp.tanh(0.7978845608028654 * (x + 0.044715 * (x * x * x))))


def _inproj_body(x_ref, sc_ref, sh_ref, w_ref, wf_ref, z_ref, f_ref, h_scr):
    j = pl.program_id(1)

    @pl.when(j == 0)
    def _():
        h = x_ref[...] * (1.0 + sc_ref[...]) + sh_ref[...]
        h_scr[...] = h.astype(BF16)
        f_ref[...] = jnp.dot(h, wf_ref[...], precision=HIGHEST, preferred_element_type=F32)

    acc = jnp.dot(h_scr[...], w_ref[...], preferred_element_type=F32)
    n_sig = (2 * D_MODEL) // TN_IN
    n_gelu = (2 * A_WIDTH) // TN_IN

    @pl.when(j < n_sig)
    def _():
        z_ref[...] = jax.nn.sigmoid(acc).astype(BF16)

    @pl.when(jnp.logical_and(j >= n_sig, j < n_sig + n_gelu))
    def _():
        z_ref[...] = _gelu_tanh(acc).astype(BF16)

    @pl.when(j >= n_sig + n_gelu)
    def _():
        z_ref[...] = acc.astype(BF16)


def _inproj(x, sc, sh, w_main, w_f):
    s, d = x.shape
    tm = min(TM_IN, s)
    return pl.pallas_call(
        _inproj_body,
        grid=(s // tm, Z_COLS // TN_IN),
        in_specs=[
            pl.BlockSpec((tm, d), lambda i, j: (i, 0)),
            pl.BlockSpec((1, d), lambda i, j: (0, 0)),
            pl.BlockSpec((1, d), lambda i, j: (0, 0)),
            pl.BlockSpec((d, TN_IN), lambda i, j: (0, j)),
            pl.BlockSpec((d, B_HEADS), lambda i, j: (0, 0)),
        ],
        out_specs=[
            pl.BlockSpec((tm, TN_IN), lambda i, j: (i, j)),
            pl.BlockSpec((tm, B_HEADS), lambda i, j: (i, 0)),
        ],
        out_shape=[
            jax.ShapeDtypeStruct((s, Z_COLS), BF16),
            jax.ShapeDtypeStruct((s, B_HEADS), F32),
        ],
        scratch_shapes=[pltpu.VMEM((tm, d), BF16)],
        compiler_params=_cparams(("arbitrary", "arbitrary")),
        name="inproj",
    )(x, sc, sh, w_main, w_f)


def _cum_body(f_ref, bf_ref, o_ref):
    n_chunks = f_ref.shape[1] // CUM_CHUNK
    row = lax.broadcasted_iota(jnp.int32, (CUM_CHUNK, CUM_CHUNK), 0)
    col = lax.broadcasted_iota(jnp.int32, (CUM_CHUNK, CUM_CHUNK), 1)
    upper = jnp.where(row <= col, 1.0, 0.0).astype(F32)

    def body(i, carry):
        sl = pl.ds(pl.multiple_of(i * CUM_CHUNK, CUM_CHUNK), CUM_CHUNK)
        xf = f_ref[:, sl] + bf_ref[...]
        logf = jnp.minimum(xf, 0.0) - jnp.log1p(jnp.exp(-jnp.abs(xf)))
        cs = jnp.dot(logf, upper, precision=HIGHEST, preferred_element_type=F32) + carry
        o_ref[:, sl] = cs
        return cs[:, CUM_CHUNK - 1:CUM_CHUNK]

    lax.fori_loop(0, n_chunks, body, jnp.zeros((B_HEADS, 1), F32))


def _forget_cumsum(f_t, b_f_col):
    h, s = f_t.shape
    return pl.pallas_call(
        _cum_body,
        out_shape=jax.ShapeDtypeStruct((h, s), F32),
        compiler_params=pltpu.CompilerParams(vmem_limit_bytes=VMEM_LIMIT),
        name="forget_cumsum",
    )(f_t, b_f_col)


def _sgu_body(u_ref, v_ref, g_ref, w_ref, b_ref, o_ref):
    tm = u_ref.shape[0]
    row = lax.broadcasted_iota(jnp.int32, (CHUNK, CHUNK), 0)
    col = lax.broadcasted_iota(jnp.int32, (CHUNK, CHUNK), 1)
    causal = row >= col
    for g in range(A_GROUPS):
        cols = slice(g * CHUNK, (g + 1) * CHUNK)
        w = jnp.where(causal, w_ref[g], 0.0).astype(BF16)
        bias = b_ref[:, g:g + 1]
        gain = g_ref[:, cols]
        for ch in range(tm // CHUNK):
            rows = slice(ch * CHUNK, (ch + 1) * CHUNK)
            v = v_ref[rows, cols].astype(F32)
            mu = jnp.mean(v, axis=-1, keepdims=True)
            vc = v - mu
            var = jnp.mean(vc * vc, axis=-1, keepdims=True)
            vn = vc * lax.rsqrt(var + LN_EPS) * gain
            mixed = jnp.dot(w, vn.astype(BF16), preferred_element_type=F32) + bias
            o_ref[rows, cols] = (u_ref[rows, cols].astype(F32) * mixed).astype(BF16)


def _sgu(z, ln_g, w_s, b_s_t):
    s = z.shape[0]
    tm = min(TM_SGU, s)
    return pl.pallas_call(
        _sgu_body,
        grid=(s // tm,),
        in_specs=[
            pl.BlockSpec((tm, A_WIDTH), lambda i: (i, Z_U // A_WIDTH)),
            pl.BlockSpec((tm, A_WIDTH), lambda i: (i, Z_V // A_WIDTH)),
            pl.BlockSpec((1, A_WIDTH), lambda i: (0, 0)),
            pl.BlockSpec((A_GROUPS, CHUNK, CHUNK), lambda i: (0, 0, 0)),
            pl.BlockSpec((CHUNK, A_GROUPS), lambda i: (0, 0)),
        ],
        out_specs=pl.BlockSpec((tm, A_WIDTH), lambda i: (i, 0)),
        out_shape=jax.ShapeDtypeStruct((s, A_WIDTH), BF16),
        compiler_params=_cparams(("arbitrary",)),
        name="sgu",
    )(z, z, ln_g, w_s, b_s_t)


def _attn_body(lo_ref, q_ref, k_ref, v_ref, ck_ref, o_ref, m_scr, acc_scr):
    h = pl.program_id(0)
    i = pl.program_id(1)
    tq = q_ref.shape[0]
    tk = tq
    scale = B_HEAD_DIM ** -0.5
    q = q_ref[...]
    c_ref = ck_ref[0, :, pl.ds(pl.multiple_of(i * tq, tq), LANES)][:, 0:1]
    ones = jnp.ones((tk, B_HEAD_DIM), BF16)

    m_scr[...] = jnp.full((tq, 1), MASK_VALUE, F32)
    acc_scr[...] = jnp.zeros((tq, 2 * B_HEAD_DIM), F32)

    def step(j, masked):
        ks = pl.ds(pl.multiple_of(j * tk, tk), tk)
        k = k_ref[ks, :]
        v_aug = jnp.concatenate([v_ref[ks, :], ones], axis=1)
        s = lax.dot_general(q, k, (((1,), (1,)), ((), ())), preferred_element_type=F32)
        s = s * scale + (c_ref - ck_ref[0, :, ks])
        if masked:
            r = lax.broadcasted_iota(jnp.int32, (tq, tk), 0)
            c = lax.broadcasted_iota(jnp.int32, (tq, tk), 1)
            s = jnp.where(r >= c, s, MASK_VALUE)
        m_prev = m_scr[...]
        m_new = jnp.maximum(m_prev, jnp.max(s, axis=1, keepdims=True))
        p = jnp.exp(s - m_new)
        alpha = jnp.exp(m_prev - m_new)
        acc_scr[...] = alpha * acc_scr[...] + jnp.dot(p.astype(BF16), v_aug, preferred_element_type=F32)
        m_scr[...] = m_new

    def body(j, carry):
        step(j, False)
        return carry

    lax.fori_loop(lo_ref[h, i], i, body, 0)
    step(i, True)
    acc = acc_scr[...]
    o_ref[...] = (acc[:, :B_HEAD_DIM] / acc[:, B_HEAD_DIM:]).astype(BF16)


def _attention(z, cum_hs, lo):
    s = z.shape[0]
    tq = min(TQ, s)
    grid_spec = pltpu.PrefetchScalarGridSpec(
        num_scalar_prefetch=1,
        grid=(B_HEADS, s // tq),
        in_specs=[
            pl.BlockSpec((tq, B_HEAD_DIM), lambda h, i, lo: (i, Z_Q // B_HEAD_DIM + h)),
            pl.BlockSpec((s, B_HEAD_DIM), lambda h, i, lo: (0, Z_K // B_HEAD_DIM + h)),
            pl.BlockSpec((s, B_HEAD_DIM), lambda h, i, lo: (0, Z_VB // B_HEAD_DIM + h)),
            pl.BlockSpec((1, 1, s), lambda h, i, lo: (h, 0, 0)),
        ],
        out_specs=pl.BlockSpec((tq, B_HEAD_DIM), lambda h, i, lo: (i, h)),
        scratch_shapes=[
            pltpu.VMEM((tq, 1), F32),
            pltpu.VMEM((tq, 2 * B_HEAD_DIM), F32),
        ],
    )
    return pl.pallas_call(
        _attn_body,
        grid_spec=grid_spec,
        out_shape=jax.ShapeDtypeStruct((s, B_WIDTH), BF16),
        compiler_params=_cparams(("arbitrary", "arbitrary")),
        name="fox_attention",
    )(lo, z, z, z, cum_hs.reshape(B_HEADS, 1, s))


def _prune_bounds(z, cum_hs):
    s = z.shape[0]
    tq = min(TQ, s)
    nblk = s // tq
    scale = B_HEAD_DIM ** -0.5

    def block_norm(col0):
        t = z[:, col0:col0 + B_WIDTH].astype(F32).reshape(nblk, tq, B_HEADS, B_HEAD_DIM)
        return jnp.sqrt(jnp.max(jnp.sum(t * t, axis=-1), axis=1)).T

    qn = block_norm(Z_Q)
    kn = block_norm(Z_K)
    kpm = lax.cummax(kn, axis=1)
    c_first = cum_hs[:, ::tq]
    c_last = cum_hs[:, tq - 1::tq]
    dot_bound = scale * qn[:, :, None] * (kpm[:, None, :] + kn[:, :, None]) * (1.0 + 1e-3)
    bound = dot_bound + c_first[:, :, None] - c_last[:, None, :]
    tile = jnp.arange(nblk, dtype=jnp.int32)
    skip = jnp.logical_and(bound <= -PRUNE_MARGIN, tile[None, None, :] < tile[None, :, None])
    return jnp.sum(skip.astype(jnp.int32), axis=-1)


def _layer_norm(y, g, b):
    mu = jnp.mean(y, axis=-1, keepdims=True)
    yc = y - mu
    var = jnp.mean(yc * yc, axis=-1, keepdims=True)
    return yc * lax.rsqrt(var + LN_EPS) * g + b


def _post_body(ya_ref, yb_ref, ga_ref, gb_ref, x_ref, wa_ref, wb_ref, wo_ref, g1_ref, lng_ref,
               lnb_ref, sc_ref, sh_ref, rw_ref, rb_ref, x1_ref, h2_ref, ti_ref, tw_ref):
    tm = x_ref.shape[0]
    a = jnp.dot(ya_ref[...], wa_ref[...], preferred_element_type=F32)
    b = jnp.dot(yb_ref[...], wb_ref[...], preferred_element_type=F32)
    merged = ga_ref[...].astype(F32) * a + gb_ref[...].astype(F32) * b
    mix = jnp.dot(merged.astype(BF16), wo_ref[...], preferred_element_type=F32)
    x1 = _layer_norm(DEEPNORM_ALPHA * x_ref[...] + (1.0 + g1_ref[...]) * mix,
                     lng_ref[...], lnb_ref[...])
    x1_ref[...] = x1
    h2 = x1 * (1.0 + sc_ref[...]) + sh_ref[...]
    h2_ref[...] = h2
    logits = jnp.dot(h2, rw_ref[...], precision=HIGHEST, preferred_element_type=F32) + rb_ref[...]
    lane = lax.broadcasted_iota(jnp.int32, (tm, ROUTER_PAD), 1)
    lane_f = lane.astype(F32)
    neg_inf = float("-inf")
    work = logits
    vals, idxs = [], []
    for _ in range(TOP_K):
        m = jnp.max(work, axis=1, keepdims=True)
        idx = jnp.min(jnp.where(work == m, lane_f, float(ROUTER_PAD)), axis=1, keepdims=True)
        vals.append(m)
        idxs.append(idx)
        work = jnp.where(lane_f == idx, neg_inf, work)
    exps = [jnp.exp(v - vals[0]) for v in vals]
    denom = exps[0] + exps[1] + exps[2] + exps[3]
    ti = jnp.zeros((tm, ROUTER_PAD), F32)
    tw = jnp.zeros((tm, ROUTER_PAD), F32)
    for k in range(TOP_K):
        ti = jnp.where(lane == k, idxs[k], ti)
        tw = jnp.where(lane == k, exps[k] / denom, tw)
    ti_ref[...] = ti.astype(jnp.int32)
    tw_ref[...] = tw


def _post_mixer(ya, yb, z, x, wa, wb, wo, g1, ln_g, ln_b, sc2, sh2, rw_pad, rb_pad):
    s, d = x.shape
    tm = min(TM_POST, s)
    row = lambda i: (i, 0)
    const = lambda i: (0, 0)
    single = pl.Buffered(1)
    return pl.pallas_call(
        _post_body,
        grid=(s // tm,),
        in_specs=[
            pl.BlockSpec((tm, A_WIDTH), row),
            pl.BlockSpec((tm, B_WIDTH), row),
            pl.BlockSpec((tm, d), lambda i: (i, Z_GA // D_MODEL)),
            pl.BlockSpec((tm, d), lambda i: (i, Z_GB // D_MODEL)),
            pl.BlockSpec((tm, d), row),
            pl.BlockSpec((A_WIDTH, d), const, pipeline_mode=single),
            pl.BlockSpec((B_WIDTH, d), const, pipeline_mode=single),
            pl.BlockSpec((d, d), const, pipeline_mode=single),
            pl.BlockSpec((1, d), const),
            pl.BlockSpec((1, d), const),
            pl.BlockSpec((1, d), const),
            pl.BlockSpec((1, d), const),
            pl.BlockSpec((1, d), const),
            pl.BlockSpec((d, ROUTER_PAD), const),
            pl.BlockSpec((1, ROUTER_PAD), const),
        ],
        out_specs=[
            pl.BlockSpec((tm, d), row),
            pl.BlockSpec((tm, d), row),
            pl.BlockSpec((tm, ROUTER_PAD), row),
            pl.BlockSpec((tm, ROUTER_PAD), row),
        ],
        out_shape=[
            jax.ShapeDtypeStruct((s, d), F32),
            jax.ShapeDtypeStruct((s, d), F32),
            jax.ShapeDtypeStruct((s, ROUTER_PAD), jnp.int32),
            jax.ShapeDtypeStruct((s, ROUTER_PAD), F32),
        ],
        compiler_params=_cparams(("arbitrary",)),
        name="post_mixer",
    )(ya, yb, z, z, x, wa, wb, wo, g1, ln_g, ln_b, sc2, sh2, rw_pad, rb_pad)


def _row_copy(src_hbm, dst_vmem, src_row, dst_row, sem):
    return pltpu.make_async_copy(src_hbm.at[pl.ds(src_row, 1)], dst_vmem.at[pl.ds(dst_row, 1)], sem)


def _moe_body(te_ref, nv_ref, rows_ref, rows_next_ref, h_hbm, wg_ref, bg_ref, wu_ref, bu_ref, wd_ref,
              bd_ref, y_ref, xbuf, wg16, wu16, wd16, sem):
    i = pl.program_id(0)
    tm = xbuf.shape[1]
    slot = lax.rem(i, 2)

    def gather(idx_ref, dst_slot):
        def issue(r, carry):
            _row_copy(h_hbm, xbuf.at[dst_slot], idx_ref[0, 0, r], r, sem.at[dst_slot]).start()
            return carry

        lax.fori_loop(0, tm, issue, 0, unroll=8)

    @pl.when(i == 0)
    def _():
        gather(rows_ref, 0)

    @pl.when(i + 1 < nv_ref[0])
    def _():
        gather(rows_next_ref, 1 - slot)

    @pl.when(i < nv_ref[0])
    def _():
        new_expert = jnp.logical_or(i == 0, te_ref[i] != te_ref[jnp.maximum(i - 1, 0)])

        @pl.when(new_expert)
        def _():
            wg16[...] = wg_ref[0].astype(BF16)
            wu16[...] = wu_ref[0].astype(BF16)
            wd16[...] = wd_ref[0].astype(BF16)

        def drain(r, carry):
            _row_copy(h_hbm, xbuf.at[slot], 0, r, sem.at[slot]).wait()
            return carry

        lax.fori_loop(0, tm, drain, 0, unroll=8)

        xb = xbuf[slot].astype(BF16)
        g = jnp.minimum(jnp.dot(xb, wg16[...], preferred_element_type=F32) + bg_ref[0], SWIGLU_LIMIT)
        u = jnp.clip(jnp.dot(xb, wu16[...], preferred_element_type=F32) + bu_ref[0],
                     -SWIGLU_LIMIT, SWIGLU_LIMIT)
        act = g * jax.nn.sigmoid(SWIGLU_ALPHA * g) * (u + 1.0)
        y_ref[...] = jnp.dot(act.astype(BF16), wd16[...], preferred_element_type=F32) + bd_ref[0]

    @pl.when(i >= nv_ref[0])
    def _():
        y_ref[...] = jnp.zeros(y_ref.shape, F32)


def _moe_experts(layer, tile_expert, n_valid, row_token, h2, wg, bg, wu, bu, wd, bd):
    s, d = h2.shape
    n_tiles = row_token.shape[0]
    tm = row_token.shape[2]
    wmap = lambda i, te, nv: (layer * N_EXPERTS + te[i], 0, 0)
    grid_spec = pltpu.PrefetchScalarGridSpec(
        num_scalar_prefetch=2,
        grid=(n_tiles,),
        in_specs=[
            pl.BlockSpec((1, 1, tm), lambda i, te, nv: (i, 0, 0), memory_space=pltpu.SMEM),
            pl.BlockSpec((1, 1, tm), lambda i, te, nv: (jnp.minimum(i + 1, n_tiles - 1), 0, 0),
                         memory_space=pltpu.SMEM),
            pl.BlockSpec(memory_space=pl.ANY),
            pl.BlockSpec((1, d, D_EXPERT), wmap),
            pl.BlockSpec((1, 1, D_EXPERT), wmap),
            pl.BlockSpec((1, d, D_EXPERT), wmap),
            pl.BlockSpec((1, 1, D_EXPERT), wmap),
            pl.BlockSpec((1, D_EXPERT, d), wmap),
            pl.BlockSpec((1, 1, d), wmap),
        ],
        out_specs=pl.BlockSpec((tm, d), lambda i, te, nv: (i, 0)),
        scratch_shapes=[
            pltpu.VMEM((2, tm, d), F32),
            pltpu.VMEM((d, D_EXPERT), BF16),
            pltpu.VMEM((d, D_EXPERT), BF16),
            pltpu.VMEM((D_EXPERT, d), BF16),
            pltpu.SemaphoreType.DMA((2,)),
        ],
    )
    return pl.pallas_call(
        _moe_body,
        grid_spec=grid_spec,
        out_shape=jax.ShapeDtypeStruct((n_tiles * tm, d), F32),
        compiler_params=_cparams(("arbitrary",)),
        name="moe_experts",
    )(tile_expert, n_valid, row_token, row_token, h2, wg, bg, wu, bu, wd, bd)


def _combine_body(pos_ref, pos_next_ref, tw_ref, x_ref, g2_ref, lng_ref, lnb_ref, y_hbm, o_ref, ybuf, sem):
    i = pl.program_id(0)
    tm = x_ref.shape[0]
    slot = lax.rem(i, 2)

    def gather(idx_ref, dst_slot):
        def issue(t, carry):
            for k in range(TOP_K):
                _row_copy(y_hbm, ybuf.at[dst_slot, k], idx_ref[0, 0, t * TOP_K + k], t,
                          sem.at[dst_slot]).start()
            return carry

        lax.fori_loop(0, tm, issue, 0, unroll=4)

    @pl.when(i == 0)
    def _():
        gather(pos_ref, 0)

    @pl.when(i + 1 < pl.num_programs(0))
    def _():
        gather(pos_next_ref, 1 - slot)

    def drain(t, carry):
        for k in range(TOP_K):
            _row_copy(y_hbm, ybuf.at[slot, k], 0, t, sem.at[slot]).wait()
        return carry

    lax.fori_loop(0, tm, drain, 0, unroll=4)

    tw = tw_ref[...]
    ffn = tw[:, 0:1] * ybuf[slot, 0]
    for k in range(1, TOP_K):
        ffn = ffn + tw[:, k:k + 1] * ybuf[slot, k]
    o_ref[...] = _layer_norm(DEEPNORM_ALPHA * x_ref[...] + (1.0 + g2_ref[...]) * ffn,
                             lng_ref[...], lnb_ref[...])


def _combine(pos, tw, x1, g2, ln_g, ln_b, y_sorted):
    s, d = x1.shape
    tm = min(TM_COMB, s)
    n_steps = s // tm
    row = lambda i: (i, 0)
    const = lambda i: (0, 0)
    pos3 = pos.reshape(n_steps, 1, tm * TOP_K)
    return pl.pallas_call(
        _combine_body,
        grid=(n_steps,),
        in_specs=[
            pl.BlockSpec((1, 1, tm * TOP_K), lambda i: (i, 0, 0), memory_space=pltpu.SMEM),
            pl.BlockSpec((1, 1, tm * TOP_K), lambda i: (jnp.minimum(i + 1, n_steps - 1), 0, 0),
                         memory_space=pltpu.SMEM),
            pl.BlockSpec((tm, ROUTER_PAD), row),
            pl.BlockSpec((tm, d), row),
            pl.BlockSpec((1, d), const),
            pl.BlockSpec((1, d), const),
            pl.BlockSpec((1, d), const),
            pl.BlockSpec(memory_space=pl.ANY),
        ],
        out_specs=pl.BlockSpec((tm, d), row),
        out_shape=jax.ShapeDtypeStruct((s, d), F32),
        scratch_shapes=[pltpu.VMEM((2, TOP_K, tm, d), F32), pltpu.SemaphoreType.DMA((2,))],
        compiler_params=_cparams(("arbitrary",)),
        name="moe_combine",
    )(pos3, pos3, tw, x1, g2, ln_g, ln_b, y_sorted)


def _routing_tables(top_i, s):
    tm = TM_MOE
    n_pairs = s * TOP_K
    n_tiles = n_pairs // tm + N_EXPERTS
    flat_e = top_i.reshape(n_pairs)
    onehot = (flat_e[:, None] == jnp.arange(N_EXPERTS, dtype=jnp.int32)[None, :]).astype(jnp.int32)
    csum = jnp.cumsum(onehot, axis=0)
    rank = jnp.sum(csum * onehot, axis=1) - 1
    counts = csum[-1]
    padded = ((counts + tm - 1) // tm) * tm
    ends = jnp.cumsum(padded)
    starts = ends - padded
    pos = jnp.sum(onehot * starts[None, :], axis=1) + rank
    row_token = jnp.zeros((n_tiles * tm,), jnp.int32).at[pos].set(
        jnp.arange(n_pairs, dtype=jnp.int32) // TOP_K)
    tile_start = jnp.arange(n_tiles, dtype=jnp.int32) * tm
    tile_expert = jnp.sum((tile_start[:, None] >= ends[None, :]).astype(jnp.int32), axis=1)
    tile_expert = jnp.minimum(tile_expert, N_EXPERTS - 1).astype(jnp.int32)
    n_valid = (ends[-1] // tm).astype(jnp.int32).reshape(1)
    return pos.astype(jnp.int32), row_token.reshape(n_tiles, 1, tm), tile_expert, n_valid


def kernel(x, c, w_ada, b_ada, w_in, b_f, sgu_ln_g, spatial_w, spatial_b, w_proj_a, w_proj_b, w_o,
           ln1_g, ln1_b, router_w, router_b, w_gate, b_gate, w_up, b_up, w_down, b_down, ln2_g, ln2_b):
    bsz, s, d = x.shape
    assert bsz == 1
    depth = w_ada.shape[0]
    xs = x.reshape(s, d)
    mod = _adaln(c.reshape(d, 1), w_ada, b_ada)

    n_a = 2 * A_WIDTH
    n_b = 3 * B_WIDTH
    w_main = jnp.concatenate(
        [w_in[:, :, n_a + n_b + B_HEADS:], w_in[:, :, :n_a + n_b]], axis=-1).astype(BF16)
    w_f = w_in[:, :, n_a + n_b:n_a + n_b + B_HEADS]
    wa16 = w_proj_a.astype(BF16)
    wb16 = w_proj_b.astype(BF16)
    wo16 = w_o.astype(BF16)
    rw_pad = jnp.pad(router_w, ((0, 0), (0, 0), (0, ROUTER_PAD - N_EXPERTS)))
    rb_pad = jnp.pad(router_b, ((0, 0), (0, ROUTER_PAD - N_EXPERTS)), constant_values=float("-inf"))

    n_all = depth * N_EXPERTS
    wg_all = w_gate.reshape(n_all, d, D_EXPERT)
    wu_all = w_up.reshape(n_all, d, D_EXPERT)
    wd_all = w_down.reshape(n_all, D_EXPERT, d)
    bg_all = b_gate.reshape(n_all, 1, D_EXPERT)
    bu_all = b_up.reshape(n_all, 1, D_EXPERT)
    bd_all = b_down.reshape(n_all, 1, d)

    for l in range(depth):
        sh1, sc1, g1, sh2, sc2, g2 = [mod[l, :, k * d:(k + 1) * d] for k in range(6)]
        z, f_logit = _inproj(xs, sc1, sh1, w_main[l], w_f[l])
        cum_hs = _forget_cumsum(f_logit.T, b_f[l].reshape(B_HEADS, 1))
        ya = _sgu(z, sgu_ln_g[l].reshape(1, A_WIDTH), spatial_w[l], spatial_b[l].T)
        yb = _attention(z, cum_hs, _prune_bounds(z, cum_hs))
        x1, h2, top_i, top_w = _post_mixer(
            ya, yb, z, xs, wa16[l], wb16[l], wo16[l], g1, ln1_g[l].reshape(1, d),
            ln1_b[l].reshape(1, d), sc2, sh2, rw_pad[l], rb_pad[l].reshape(1, ROUTER_PAD))
        pos, row_token, tile_expert, n_valid = _routing_tables(top_i[:, :TOP_K], s)
        y_sorted = _moe_experts(l, tile_expert, n_valid, row_token, h2, wg_all, bg_all, wu_all, bu_all,
                                wd_all, bd_all)
        xs = _combine(pos, top_w, x1, g2, ln2_g[l].reshape(1, d), ln2_b[l].reshape(1, d), y_sorted)
    return xs.reshape(bsz, s, d)
```

```python
import functools

import jax
import jax.numpy as jnp
from jax import lax
from jax.experimental import pallas as pl
from jax.experimental.pallas import tpu as pltpu

F32 = jnp.float32
BF16 = jnp.bfloat16
HIGHEST = lax.Precision.HIGHEST

D_MODEL = 2048
DEPTH = 4
CHUNK = 128
A_GROUPS = 8
A_WIDTH = 1024
B_HEADS = 8
B_HEAD_DIM = 128
B_WIDTH = 1024
N_EXPERTS = 32
TOP_K = 4
D_EXPERT = 512
SWIGLU_LIMIT = 7.0
SWIGLU_ALPHA = 1.702
DEEPNORM_ALPHA = (2.0 * DEPTH) ** 0.25
LN_EPS = 1e-5
MASK_VALUE = -1e30

LANES = 128
VMEM_LIMIT = 56 * 1024 * 1024

Z_COLS = 2 * D_MODEL + 2 * A_WIDTH + 3 * B_WIDTH
Z_GA, Z_GB, Z_U, Z_V, Z_Q, Z_K, Z_VB = 0, 2048, 4096, 5120, 6144, 7168, 8192

TN_IN = 1024
TM_IN = 1024
TM_SGU = 256
TQ = 512
PRUNE_MARGIN = 110.0
NORM_SLACK = 1.01
TM_POST = 256
TM_MOE = 256
TM_COMB = 128
CUM_CHUNK = 512
ROUTER_PAD = 128

N_SLABS = D_MODEL // LANES
SLAB_PITCH = 20


def _store_slabs(ref, base, rows, value):
    for a in range(N_SLABS):
        ref[pl.ds(base + a, rows, stride=SLAB_PITCH), :] = value[:, a * LANES:(a + 1) * LANES]
    zeros = jnp.zeros((rows, LANES), F32)
    for a in range(N_SLABS, SLAB_PITCH):
        ref[pl.ds(base + a, rows, stride=SLAB_PITCH), :] = zeros


def _load_slab(ref, base, rows, a):
    return ref[pl.ds(base + a, rows, stride=SLAB_PITCH), :]


def _split_bf16(v):
    hi = v.astype(BF16)
    lo = (v - hi.astype(F32)).astype(BF16)
    return hi, lo


def _cparams(sem, vmem=VMEM_LIMIT):
    return pltpu.CompilerParams(dimension_semantics=sem, vmem_limit_bytes=vmem)


def _adaln_body(c_ref, w_ref, b_ref, o_ref):
    c = c_ref[...]
    cond = c * jax.nn.sigmoid(c)
    o_ref[0] = jnp.sum(w_ref[0] * cond, axis=0, keepdims=True) + b_ref[0]


def _adaln(c_col, w_ada, b_ada):
    depth, d, n = w_ada.shape
    tn = 1024
    return pl.pallas_call(
        _adaln_body,
        grid=(depth, n // tn),
        in_specs=[
            pl.BlockSpec((d, 1), lambda l, j: (0, 0)),
            pl.BlockSpec((1, d, tn), lambda l, j: (l, 0, j)),
            pl.BlockSpec((1, 1, tn), lambda l, j: (l, 0, j)),
        ],
        out_specs=pl.BlockSpec((1, 1, tn), lambda l, j: (l, 0, j)),
        out_shape=jax.ShapeDtypeStruct((depth, 1, n), F32),
        compiler_params=_cparams(("arbitrary", "arbitrary")),
        name="adaln",
    )(c_col, w_ada, b_ada.reshape(depth, 1, n))


def _gelu_tanh(x):
    return 0.5 * x * (1.0 + jnp.tanh(0.7978845608028654 * (x + 0.044715 * (x * x * x))))


def _inproj_body(x_ref, sc_ref, sh_ref, w_ref, wf_ref, z_ref, f_ref, qn_ref, kn_ref, h_scr):
    j = pl.program_id(1)

    @pl.when(j == 0)
    def _():
        h = x_ref[...] * (1.0 + sc_ref[...]) + sh_ref[...]
        h_hi, h_lo = _split_bf16(h)
        h_scr[...] = h_hi
        r1 = jnp.dot(h_hi, wf_ref[...], preferred_element_type=F32)
        r2 = jnp.dot(h_lo, wf_ref[:, :LANES], preferred_element_type=F32)
        f_ref[...] = (r1[:, :LANES] + r1[:, LANES:] + r2)[:, :B_HEADS]

    acc = jnp.dot(h_scr[...], w_ref[...], preferred_element_type=F32)
    n_sig = (2 * D_MODEL) // TN_IN
    n_gelu = (2 * A_WIDTH) // TN_IN

    @pl.when(j < n_sig)
    def _():
        z_ref[...] = jax.nn.sigmoid(acc).astype(BF16)

    @pl.when(jnp.logical_and(j >= n_sig, j < n_sig + n_gelu))
    def _():
        z_ref[...] = _gelu_tanh(acc).astype(BF16)

    @pl.when(j >= n_sig + n_gelu)
    def _():
        z_ref[...] = acc.astype(BF16)

    def head_sumsq():
        zf = acc.astype(BF16).astype(F32)
        c = lax.broadcasted_iota(jnp.int32, (TN_IN, LANES), 0)
        hcol = lax.broadcasted_iota(jnp.int32, (TN_IN, LANES), 1)
        sel = jnp.where(c // B_HEAD_DIM == hcol, 1.0, 0.0).astype(BF16)
        return jnp.dot((zf * zf).astype(BF16), sel, preferred_element_type=F32)

    @pl.when(j == Z_Q // TN_IN)
    def _():
        qn_ref[...] = head_sumsq()

    @pl.when(j == Z_K // TN_IN)
    def _():
        kn_ref[...] = head_sumsq()


def _inproj(x, sc, sh, w_main, w_f):
    s, d = x.shape
    tm = min(TM_IN, s)
    return pl.pallas_call(
        _inproj_body,
        grid=(s // tm, Z_COLS // TN_IN),
        in_specs=[
            pl.BlockSpec((tm, d), lambda i, j: (i, 0)),
            pl.BlockSpec((1, d), lambda i, j: (0, 0)),
            pl.BlockSpec((1, d), lambda i, j: (0, 0)),
            pl.BlockSpec((d, TN_IN), lambda i, j: (0, j)),
            pl.BlockSpec((d, 2 * LANES), lambda i, j: (0, 0)),
        ],
        out_specs=[
            pl.BlockSpec((tm, TN_IN), lambda i, j: (i, j)),
            pl.BlockSpec((tm, B_HEADS), lambda i, j: (i, 0)),
            pl.BlockSpec((tm, LANES), lambda i, j: (i, 0)),
            pl.BlockSpec((tm, LANES), lambda i, j: (i, 0)),
        ],
        out_shape=[
            jax.ShapeDtypeStruct((s, Z_COLS), BF16),
            jax.ShapeDtypeStruct((s, B_HEADS), F32),
            jax.ShapeDtypeStruct((s, LANES), F32),
            jax.ShapeDtypeStruct((s, LANES), F32),
        ],
        scratch_shapes=[pltpu.VMEM((tm, d), BF16)],
        compiler_params=_cparams(("arbitrary", "arbitrary")),
        name="inproj",
    )(x, sc, sh, w_main, w_f)


def _cum_body(f_ref, bf_ref, o_ref):
    n_chunks = f_ref.shape[1] // CUM_CHUNK
    row = lax.broadcasted_iota(jnp.int32, (CUM_CHUNK, CUM_CHUNK), 0)
    col = lax.broadcasted_iota(jnp.int32, (CUM_CHUNK, CUM_CHUNK), 1)
    upper = jnp.where(row <= col, 1.0, 0.0).astype(F32)

    def body(i, carry):
        sl = pl.ds(pl.multiple_of(i * CUM_CHUNK, CUM_CHUNK), CUM_CHUNK)
        xf = f_ref[:, sl] + bf_ref[...]
        logf = jnp.minimum(xf, 0.0) - jnp.log1p(jnp.exp(-jnp.abs(xf)))
        cs = jnp.dot(logf, upper, precision=HIGHEST, preferred_element_type=F32) + carry
        o_ref[:, sl] = cs
        return cs[:, CUM_CHUNK - 1:CUM_CHUNK]

    lax.fori_loop(0, n_chunks, body, jnp.zeros((B_HEADS, 1), F32))


def _forget_cumsum(f_t, b_f_col):
    h, s = f_t.shape
    return pl.pallas_call(
        _cum_body,
        out_shape=jax.ShapeDtypeStruct((h, s), F32),
        compiler_params=pltpu.CompilerParams(vmem_limit_bytes=VMEM_LIMIT),
        name="forget_cumsum",
    )(f_t, b_f_col)


def _sgu_body(u_ref, v_ref, g_ref, w_ref, b_ref, o_ref):
    tm = u_ref.shape[0]
    row = lax.broadcasted_iota(jnp.int32, (CHUNK, CHUNK), 0)
    col = lax.broadcasted_iota(jnp.int32, (CHUNK, CHUNK), 1)
    causal = row >= col
    for g in range(A_GROUPS):
        cols = slice(g * CHUNK, (g + 1) * CHUNK)
        w = jnp.where(causal, w_ref[g], 0.0).astype(BF16)
        bias = b_ref[:, g:g + 1]
        gain = g_ref[:, cols]
        for ch in range(tm // CHUNK):
            rows = slice(ch * CHUNK, (ch + 1) * CHUNK)
            v = v_ref[rows, cols].astype(F32)
            mu = jnp.mean(v, axis=-1, keepdims=True)
            vc = v - mu
            var = jnp.mean(vc * vc, axis=-1, keepdims=True)
            vn = vc * lax.rsqrt(var + LN_EPS) * gain
            mixed = jnp.dot(w, vn.astype(BF16), preferred_element_type=F32) + bias
            o_ref[rows, cols] = (u_ref[rows, cols].astype(F32) * mixed).astype(BF16)


def _sgu(z, ln_g, w_s, b_s_t):
    s = z.shape[0]
    tm = min(TM_SGU, s)
    return pl.pallas_call(
        _sgu_body,
        grid=(s // tm,),
        in_specs=[
            pl.BlockSpec((tm, A_WIDTH), lambda i: (i, Z_U // A_WIDTH)),
            pl.BlockSpec((tm, A_WIDTH), lambda i: (i, Z_V // A_WIDTH)),
            pl.BlockSpec((1, A_WIDTH), lambda i: (0, 0)),
            pl.BlockSpec((A_GROUPS, CHUNK, CHUNK), lambda i: (0, 0, 0)),
            pl.BlockSpec((CHUNK, A_GROUPS), lambda i: (0, 0)),
        ],
        out_specs=pl.BlockSpec((tm, A_WIDTH), lambda i: (i, 0)),
        out_shape=jax.ShapeDtypeStruct((s, A_WIDTH), BF16),
        compiler_params=_cparams(("arbitrary",)),
        name="sgu",
    )(z, z, ln_g, w_s, b_s_t)


def _attn_body(lo_ref, q_ref, k_ref, v_ref, ck_ref, o_ref, m_scr, acc_scr, s_even, s_odd):
    h = pl.program_id(0)
    i = pl.program_id(1)
    tq = q_ref.shape[0]
    tk = tq
    scale = B_HEAD_DIM ** -0.5
    q = q_ref[...]
    c_ref = ck_ref[0, :, pl.ds(pl.multiple_of(i * tq, tq), LANES)][:, 0:1]
    ones = jnp.ones((tk, B_HEAD_DIM), BF16)
    lo = lo_ref[h, i]

    m_scr[...] = jnp.full((tq, 1), MASK_VALUE, F32)
    acc_scr[...] = jnp.zeros((tq, 2 * B_HEAD_DIM), F32)

    def key_slice(j):
        return pl.ds(pl.multiple_of(j * tk, tk), tk)

    def raw_scores(j):
        return lax.dot_general(q, k_ref[key_slice(j), :], (((1,), (1,)), ((), ())),
                               preferred_element_type=F32)

    def consume(s_ref, j, masked):
        ks = key_slice(j)
        v_aug = jnp.concatenate([v_ref[ks, :], ones], axis=1)
        s = s_ref[...] * scale + (c_ref - ck_ref[0, :, ks])
        if masked:
            r = lax.broadcasted_iota(jnp.int32, (tq, tk), 0)
            c = lax.broadcasted_iota(jnp.int32, (tq, tk), 1)
            s = jnp.where(r >= c, s, MASK_VALUE)
        m_prev = m_scr[...]
        m_new = jnp.maximum(m_prev, jnp.max(s, axis=1, keepdims=True))
        p = jnp.exp(s - m_new)
        alpha = jnp.exp(m_prev - m_new)
        acc_scr[...] = alpha * acc_scr[...] + jnp.dot(p.astype(BF16), v_aug, preferred_element_type=F32)
        m_scr[...] = m_new

    s_even[...] = raw_scores(lo)

    def body(j, carry):
        even_visit = lax.rem(j - lo, 2) == 0

        @pl.when(even_visit)
        def _():
            s_odd[...] = raw_scores(j + 1)
            consume(s_even, j, False)

        @pl.when(jnp.logical_not(even_visit))
        def _():
            s_even[...] = raw_scores(j + 1)
            consume(s_odd, j, False)

        return carry

    lax.fori_loop(lo, i, body, 0)
    last_even = lax.rem(i - lo, 2) == 0

    @pl.when(last_even)
    def _():
        consume(s_even, i, True)

    @pl.when(jnp.logical_not(last_even))
    def _():
        consume(s_odd, i, True)

    acc = acc_scr[...]
    o_ref[...] = (acc[:, :B_HEAD_DIM] / acc[:, B_HEAD_DIM:]).astype(BF16)


def _attention(z, cum_hs, lo):
    s = z.shape[0]
    tq = min(TQ, s)
    grid_spec = pltpu.PrefetchScalarGridSpec(
        num_scalar_prefetch=1,
        grid=(B_HEADS, s // tq),
        in_specs=[
            pl.BlockSpec((tq, B_HEAD_DIM), lambda h, i, lo: (i, Z_Q // B_HEAD_DIM + h)),
            pl.BlockSpec((s, B_HEAD_DIM), lambda h, i, lo: (0, Z_K // B_HEAD_DIM + h)),
            pl.BlockSpec((s, B_HEAD_DIM), lambda h, i, lo: (0, Z_VB // B_HEAD_DIM + h)),
            pl.BlockSpec((1, 1, s), lambda h, i, lo: (h, 0, 0)),
        ],
        out_specs=pl.BlockSpec((tq, B_HEAD_DIM), lambda h, i, lo: (i, h)),
        scratch_shapes=[
            pltpu.VMEM((tq, 1), F32),
            pltpu.VMEM((tq, 2 * B_HEAD_DIM), F32),
            pltpu.VMEM((tq, tq), F32),
            pltpu.VMEM((tq, tq), F32),
        ],
    )
    return pl.pallas_call(
        _attn_body,
        grid_spec=grid_spec,
        out_shape=jax.ShapeDtypeStruct((s, B_WIDTH), BF16),
        compiler_params=_cparams(("arbitrary", "arbitrary")),
        name="fox_attention",
    )(lo, z, z, z, cum_hs.reshape(B_HEADS, 1, s))


def _prune_bounds(q_sumsq, k_sumsq, cum_hs):
    s = cum_hs.shape[1]
    tq = min(TQ, s)
    nblk = s // tq
    scale = B_HEAD_DIM ** -0.5

    def block_norm(sumsq):
        t = sumsq[:, :B_HEADS].reshape(nblk, tq, B_HEADS)
        return jnp.sqrt(jnp.max(t, axis=1)).T

    qn = block_norm(q_sumsq)
    kn = block_norm(k_sumsq)
    kpm = lax.cummax(kn, axis=1)
    c_first = cum_hs[:, ::tq]
    c_last = cum_hs[:, tq - 1::tq]
    dot_bound = scale * qn[:, :, None] * (kpm[:, None, :] + kn[:, :, None]) * NORM_SLACK
    bound = dot_bound + c_first[:, :, None] - c_last[:, None, :]
    tile = jnp.arange(nblk, dtype=jnp.int32)
    skip = jnp.logical_and(bound <= -PRUNE_MARGIN, tile[None, None, :] < tile[None, :, None])
    return jnp.sum(skip.astype(jnp.int32), axis=-1)


def _layer_norm(y, g, b):
    mu = jnp.mean(y, axis=-1, keepdims=True)
    yc = y - mu
    var = jnp.mean(yc * yc, axis=-1, keepdims=True)
    return yc * lax.rsqrt(var + LN_EPS) * g + b


def _post_body(ya_ref, yb_ref, ga_ref, gb_ref, x_ref, wa_ref, wb_ref, wo_ref, g1_ref, lng_ref,
               lnb_ref, sc_ref, sh_ref, rw_ref, rb_ref, x1_ref, h2_ref, ti_ref, tw_ref):
    tm = x_ref.shape[0]
    a = jnp.dot(ya_ref[...], wa_ref[...], preferred_element_type=F32)
    b = jnp.dot(yb_ref[...], wb_ref[...], preferred_element_type=F32)
    merged = ga_ref[...].astype(F32) * a + gb_ref[...].astype(F32) * b
    mix = jnp.dot(merged.astype(BF16), wo_ref[...], preferred_element_type=F32)
    x1 = _layer_norm(DEEPNORM_ALPHA * x_ref[...] + (1.0 + g1_ref[...]) * mix,
                     lng_ref[...], lnb_ref[...])
    x1_ref[...] = x1
    h2 = x1 * (1.0 + sc_ref[...]) + sh_ref[...]
    _store_slabs(h2_ref, 0, tm, h2)
    h_hi, h_lo = _split_bf16(h2)
    r1 = jnp.dot(h_hi, rw_ref[...], preferred_element_type=F32)
    r2 = jnp.dot(h_lo, rw_ref[:, :ROUTER_PAD], preferred_element_type=F32)
    logits = r1[:, :ROUTER_PAD] + r1[:, ROUTER_PAD:] + r2 + rb_ref[...]
    lane = lax.broadcasted_iota(jnp.int32, (tm, ROUTER_PAD), 1)
    lane_f = lane.astype(F32)
    neg_inf = float("-inf")
    work = logits
    vals, idxs = [], []
    for _ in range(TOP_K):
        m = jnp.max(work, axis=1, keepdims=True)
        idx = jnp.min(jnp.where(work == m, lane_f, float(ROUTER_PAD)), axis=1, keepdims=True)
        vals.append(m)
        idxs.append(idx)
        work = jnp.where(lane_f == idx, neg_inf, work)
    exps = [jnp.exp(v - vals[0]) for v in vals]
    denom = exps[0] + exps[1] + exps[2] + exps[3]
    ti = jnp.zeros((tm, ROUTER_PAD), F32)
    tw = jnp.zeros((tm, ROUTER_PAD), F32)
    for k in range(TOP_K):
        ti = jnp.where(lane == k, idxs[k], ti)
        tw = jnp.where(lane == k, exps[k] / denom, tw)
    ti_ref[...] = ti.astype(jnp.int32)
    tw_ref[...] = tw


def _post_mixer(ya, yb, z, x, wa, wb, wo, g1, ln_g, ln_b, sc2, sh2, rw_pad, rb_pad):
    s, d = x.shape
    tm = min(TM_POST, s)
    row = lambda i: (i, 0)
    const = lambda i: (0, 0)
    single = pl.Buffered(1)
    return pl.pallas_call(
        _post_body,
        grid=(s // tm,),
        in_specs=[
            pl.BlockSpec((tm, A_WIDTH), row),
            pl.BlockSpec((tm, B_WIDTH), row),
            pl.BlockSpec((tm, d), lambda i: (i, Z_GA // D_MODEL)),
            pl.BlockSpec((tm, d), lambda i: (i, Z_GB // D_MODEL)),
            pl.BlockSpec((tm, d), row),
            pl.BlockSpec((A_WIDTH, d), const, pipeline_mode=single),
            pl.BlockSpec((B_WIDTH, d), const, pipeline_mode=single),
            pl.BlockSpec((d, d), const, pipeline_mode=single),
            pl.BlockSpec((1, d), const),
            pl.BlockSpec((1, d), const),
            pl.BlockSpec((1, d), const),
            pl.BlockSpec((1, d), const),
            pl.BlockSpec((1, d), const),
            pl.BlockSpec((d, 2 * ROUTER_PAD), const),
            pl.BlockSpec((1, ROUTER_PAD), const),
        ],
        out_specs=[
            pl.BlockSpec((tm, d), row),
            pl.BlockSpec((tm * SLAB_PITCH, LANES), row),
            pl.BlockSpec((tm, ROUTER_PAD), row),
            pl.BlockSpec((tm, ROUTER_PAD), row),
        ],
        out_shape=[
            jax.ShapeDtypeStruct((s, d), F32),
            jax.ShapeDtypeStruct((s * SLAB_PITCH, LANES), F32),
            jax.ShapeDtypeStruct((s, ROUTER_PAD), jnp.int32),
            jax.ShapeDtypeStruct((s, ROUTER_PAD), F32),
        ],
        compiler_params=_cparams(("arbitrary",)),
        name="post_mixer",
    )(ya, yb, z, z, x, wa, wb, wo, g1, ln_g, ln_b, sc2, sh2, rw_pad, rb_pad)


def _slab_copy(src_hbm, dst_vmem, src_row, dst_row, sem):
    return pltpu.make_async_copy(src_hbm.at[pl.ds(src_row, N_SLABS)],
                                 dst_vmem.at[pl.ds(dst_row, N_SLABS)], sem)


def _gather_rows(src_hbm, dst_vmem, idx_ref, n, dst_base, dst_row, sem, unrolled):
    if unrolled:
        for r in range(n):
            _slab_copy(src_hbm, dst_vmem, idx_ref[0, 0, r], dst_base + dst_row(r) * SLAB_PITCH, sem).start()
    else:
        def issue(r, carry):
            _slab_copy(src_hbm, dst_vmem, idx_ref[0, 0, r], dst_base + dst_row(r) * SLAB_PITCH, sem).start()
            return carry

        lax.fori_loop(0, n, issue, 0)


def _wait_rows(src_hbm, dst_vmem, n, dst_base, sem):
    def drain(r, carry):
        _slab_copy(src_hbm, dst_vmem, 0, dst_base + r * SLAB_PITCH, sem).wait()
        return carry

    lax.fori_loop(0, n, drain, 0, unroll=8)


def _moe_body(te_ref, nv_ref, rows_ref, rows_next_ref, h_hbm, wg_ref, bg_ref, wu_ref, bu_ref, wd_ref,
              bd_ref, y_ref, xbuf, wg16, wu16, wd16, sem):
    i = pl.program_id(0)
    tm = xbuf.shape[0] // (2 * SLAB_PITCH)
    slot = lax.rem(i, 2)
    slot_rows = tm * SLAB_PITCH

    same_row = lambda r: r

    @pl.when(i == 0)
    def _():
        _gather_rows(h_hbm, xbuf, rows_ref, tm, 0, same_row, sem.at[0], unrolled=False)

    @pl.when(i + 1 < nv_ref[0])
    def _():
        _gather_rows(h_hbm, xbuf, rows_next_ref, tm, (1 - slot) * slot_rows, same_row,
                     sem.at[1 - slot], unrolled=True)

    @pl.when(i < nv_ref[0])
    def _():
        new_expert = jnp.logical_or(i == 0, te_ref[i] != te_ref[jnp.maximum(i - 1, 0)])

        @pl.when(new_expert)
        def _():
            wg16[...] = wg_ref[0].astype(BF16)
            wu16[...] = wu_ref[0].astype(BF16)
            wd16[...] = wd_ref[0].astype(BF16)

        base = slot * slot_rows
        _wait_rows(h_hbm, xbuf, tm, base, sem.at[slot])
        xb = jnp.concatenate(
            [_load_slab(xbuf, base, tm, a).astype(BF16) for a in range(N_SLABS)], axis=1)
        g = jnp.minimum(jnp.dot(xb, wg16[...], preferred_element_type=F32) + bg_ref[0], SWIGLU_LIMIT)
        u = jnp.clip(jnp.dot(xb, wu16[...], preferred_element_type=F32) + bu_ref[0],
                     -SWIGLU_LIMIT, SWIGLU_LIMIT)
        act = g * jax.nn.sigmoid(SWIGLU_ALPHA * g) * (u + 1.0)
        y = jnp.dot(act.astype(BF16), wd16[...], preferred_element_type=F32) + bd_ref[0]
        _store_slabs(y_ref, 0, tm, y)

    @pl.when(i >= nv_ref[0])
    def _():
        y_ref[...] = jnp.zeros(y_ref.shape, F32)


def _moe_experts(layer, tile_expert, n_valid, row_token, h2, wg, bg, wu, bu, wd, bd):
    d = D_MODEL
    n_tiles = row_token.shape[0]
    tm = row_token.shape[2]
    wmap = lambda i, te, nv: (layer * N_EXPERTS + te[i], 0, 0)
    grid_spec = pltpu.PrefetchScalarGridSpec(
        num_scalar_prefetch=2,
        grid=(n_tiles,),
        in_specs=[
            pl.BlockSpec((1, 1, tm), lambda i, te, nv: (i, 0, 0), memory_space=pltpu.SMEM),
            pl.BlockSpec((1, 1, tm), lambda i, te, nv: (jnp.minimum(i + 1, n_tiles - 1), 0, 0),
                         memory_space=pltpu.SMEM),
            pl.BlockSpec(memory_space=pl.ANY),
            pl.BlockSpec((1, d, D_EXPERT), wmap),
            pl.BlockSpec((1, 1, D_EXPERT), wmap),
            pl.BlockSpec((1, d, D_EXPERT), wmap),
            pl.BlockSpec((1, 1, D_EXPERT), wmap),
            pl.BlockSpec((1, D_EXPERT, d), wmap),
            pl.BlockSpec((1, 1, d), wmap),
        ],
        out_specs=pl.BlockSpec((tm * SLAB_PITCH, LANES), lambda i, te, nv: (i, 0)),
        scratch_shapes=[
            pltpu.VMEM((2 * tm * SLAB_PITCH, LANES), F32),
            pltpu.VMEM((d, D_EXPERT), BF16),
            pltpu.VMEM((d, D_EXPERT), BF16),
            pltpu.VMEM((D_EXPERT, d), BF16),
            pltpu.SemaphoreType.DMA((2,)),
        ],
    )
    return pl.pallas_call(
        _moe_body,
        grid_spec=grid_spec,
        out_shape=jax.ShapeDtypeStruct((n_tiles * tm * SLAB_PITCH, LANES), F32),
        compiler_params=_cparams(("arbitrary",)),
        name="moe_experts",
    )(tile_expert, n_valid, row_token, row_token, h2, wg, bg, wu, bu, wd, bd)


def _combine_body(pos_ref, pos_next_ref, tw_ref, x_ref, g2_ref, lng_ref, lnb_ref, y_hbm, o_ref, ybuf, sem):
    i = pl.program_id(0)
    tm = x_ref.shape[0]
    n_rows = tm * TOP_K
    slot = lax.rem(i, 2)
    slot_rows = n_rows * SLAB_PITCH
    k_major = lambda r: (r % TOP_K) * tm + r // TOP_K

    @pl.when(i == 0)
    def _():
        _gather_rows(y_hbm, ybuf, pos_ref, n_rows, 0, k_major, sem.at[0], unrolled=False)

    @pl.when(i + 1 < pl.num_programs(0))
    def _():
        _gather_rows(y_hbm, ybuf, pos_next_ref, n_rows, (1 - slot) * slot_rows, k_major,
                     sem.at[1 - slot], unrolled=True)

    base = slot * slot_rows
    _wait_rows(y_hbm, ybuf, n_rows, base, sem.at[slot])

    tw = tw_ref[...]
    pieces = []
    for a in range(N_SLABS):
        acc = None
        for k in range(TOP_K):
            rows = _load_slab(ybuf, base + k * tm * SLAB_PITCH, tm, a)
            term = tw[:, k:k + 1] * rows
            acc = term if acc is None else acc + term
        pieces.append(acc)
    ffn = jnp.concatenate(pieces, axis=1)
    o_ref[...] = _layer_norm(DEEPNORM_ALPHA * x_ref[...] + (1.0 + g2_ref[...]) * ffn,
                             lng_ref[...], lnb_ref[...])


def _combine(pos, tw, x1, g2, ln_g, ln_b, y_sorted):
    s, d = x1.shape
    tm = min(TM_COMB, s)
    n_steps = s // tm
    row = lambda i: (i, 0)
    const = lambda i: (0, 0)
    pos3 = pos.reshape(n_steps, 1, tm * TOP_K)
    return pl.pallas_call(
        _combine_body,
        grid=(n_steps,),
        in_specs=[
            pl.BlockSpec((1, 1, tm * TOP_K), lambda i: (i, 0, 0), memory_space=pltpu.SMEM),
            pl.BlockSpec((1, 1, tm * TOP_K), lambda i: (jnp.minimum(i + 1, n_steps - 1), 0, 0),
                         memory_space=pltpu.SMEM),
            pl.BlockSpec((tm, ROUTER_PAD), row),
            pl.BlockSpec((tm, d), row),
            pl.BlockSpec((1, d), const),
            pl.BlockSpec((1, d), const),
            pl.BlockSpec((1, d), const),
            pl.BlockSpec(memory_space=pl.ANY),
        ],
        out_specs=pl.BlockSpec((tm, d), row),
        out_shape=jax.ShapeDtypeStruct((s, d), F32),
        scratch_shapes=[pltpu.VMEM((2 * TOP_K * tm * SLAB_PITCH, LANES), F32),
                        pltpu.SemaphoreType.DMA((2,))],
        compiler_params=_cparams(("arbitrary",)),
        name="moe_combine",
    )(pos3, pos3, tw, x1, g2, ln_g, ln_b, y_sorted)


def _routing_tables(top_i, s):
    tm = TM_MOE
    n_pairs = s * TOP_K
    n_tiles = n_pairs // tm + N_EXPERTS
    flat_e = top_i.reshape(n_pairs)
    onehot = (flat_e[:, None] == jnp.arange(N_EXPERTS, dtype=jnp.int32)[None, :]).astype(jnp.int32)
    csum = jnp.cumsum(onehot, axis=0)
    rank = jnp.sum(csum * onehot, axis=1) - 1
    counts = csum[-1]
    padded = ((counts + tm - 1) // tm) * tm
    ends = jnp.cumsum(padded)
    starts = ends - padded
    pos = jnp.sum(onehot * starts[None, :], axis=1) + rank
    row_token = jnp.zeros((n_tiles * tm,), jnp.int32).at[pos].set(
        (jnp.arange(n_pairs, dtype=jnp.int32) // TOP_K) * SLAB_PITCH)
    tile_start = jnp.arange(n_tiles, dtype=jnp.int32) * tm
    tile_expert = jnp.sum((tile_start[:, None] >= ends[None, :]).astype(jnp.int32), axis=1)
    tile_expert = jnp.minimum(tile_expert, N_EXPERTS - 1).astype(jnp.int32)
    n_valid = (ends[-1] // tm).astype(jnp.int32).reshape(1)
    pos_flat = (pos * SLAB_PITCH).astype(jnp.int32)
    return pos_flat, row_token.reshape(n_tiles, 1, tm), tile_expert, n_valid


def kernel(x, c, w_ada, b_ada, w_in, b_f, sgu_ln_g, spatial_w, spatial_b, w_proj_a, w_proj_b, w_o,
           ln1_g, ln1_b, router_w, router_b, w_gate, b_gate, w_up, b_up, w_down, b_down, ln2_g, ln2_b):
    bsz, s, d = x.shape
    assert bsz == 1
    depth = w_ada.shape[0]
    xs = x.reshape(s, d)
    mod = _adaln(c.reshape(d, 1), w_ada, b_ada)

    n_a = 2 * A_WIDTH
    n_b = 3 * B_WIDTH
    w_main = jnp.concatenate(
        [w_in[:, :, n_a + n_b + B_HEADS:], w_in[:, :, :n_a + n_b]], axis=-1).astype(BF16)

    def split_pad(w):
        hi = w.astype(BF16)
        lo = (w - hi.astype(F32)).astype(BF16)
        pad = ((0, 0), (0, 0), (0, LANES - w.shape[-1]))
        return jnp.concatenate([jnp.pad(hi, pad), jnp.pad(lo, pad)], axis=-1)

    w_f = split_pad(w_in[:, :, n_a + n_b:n_a + n_b + B_HEADS])
    wa16 = w_proj_a.astype(BF16)
    wb16 = w_proj_b.astype(BF16)
    wo16 = w_o.astype(BF16)
    rw_pad = split_pad(router_w)
    rb_pad = jnp.pad(router_b, ((0, 0), (0, ROUTER_PAD - N_EXPERTS)), constant_values=float("-inf"))

    n_all = depth * N_EXPERTS
    wg_all = w_gate.reshape(n_all, d, D_EXPERT)
    wu_all = w_up.reshape(n_all, d, D_EXPERT)
    wd_all = w_down.reshape(n_all, D_EXPERT, d)
    bg_all = b_gate.reshape(n_all, 1, D_EXPERT)
    bu_all = b_up.reshape(n_all, 1, D_EXPERT)
    bd_all = b_down.reshape(n_all, 1, d)

    for l in range(depth):
        sh1, sc1, g1, sh2, sc2, g2 = [mod[l, :, k * d:(k + 1) * d] for k in range(6)]
        z, f_logit, q_sumsq, k_sumsq = _inproj(xs, sc1, sh1, w_main[l], w_f[l])
        cum_hs = _forget_cumsum(f_logit.T, b_f[l].reshape(B_HEADS, 1))
        ya = _sgu(z, sgu_ln_g[l].reshape(1, A_WIDTH), spatial_w[l], spatial_b[l].T)
        yb = _attention(z, cum_hs, _prune_bounds(q_sumsq, k_sumsq, cum_hs))
        x1, h2, top_i, top_w = _post_mixer(
            ya, yb, z, xs, wa16[l], wb16[l], wo16[l], g1, ln1_g[l].reshape(1, d),
            ln1_b[l].reshape(1, d), sc2, sh2, rw_pad[l], rb_pad[l].reshape(1, ROUTER_PAD))
        pos, row_token, tile_expert, n_valid = _routing_tables(top_i[:, :TOP_K], s)
        y_sorted = _moe_experts(l, tile_expert, n_valid, row_token, h2, wg_all, bg_all, wu_all, bu_all,
                                wd_all, bd_all)
        xs = _combine(pos, top_w, x1, g2, ln2_g[l].reshape(1, d), ln2_b[l].reshape(1, d), y_sorted)
    return xs.reshape(bsz, s, d)
```

```python
import functools

import jax
import jax.numpy as jnp
from jax import lax
from jax.experimental import pallas as pl
from jax.experimental.pallas import tpu as pltpu

F32 = jnp.float32
BF16 = jnp.bfloat16
HIGHEST = lax.Precision.HIGHEST

D_MODEL = 2048
DEPTH = 4
CHUNK = 128
A_GROUPS = 8
A_WIDTH = 1024
B_HEADS = 8
B_HEAD_DIM = 128
B_WIDTH = 1024
N_EXPERTS = 32
TOP_K = 4
D_EXPERT = 512
SWIGLU_LIMIT = 7.0
SWIGLU_ALPHA = 1.702
DEEPNORM_ALPHA = (2.0 * DEPTH) ** 0.25
LN_EPS = 1e-5
MASK_VALUE = -1e30

LANES = 128
VMEM_LIMIT = 56 * 1024 * 1024

Z_COLS = 2 * D_MODEL + 2 * A_WIDTH + 3 * B_WIDTH
Z_GA, Z_GB, Z_U, Z_V, Z_Q, Z_K, Z_VB = 0, 2048, 4096, 5120, 6144, 7168, 8192

TN_IN = 1024
TN_SUB = 256
TM_IN = 1024
TM_SGU = 256
TQ = 512
PRUNE_MARGIN = 110.0
NORM_SLACK = 1.01
ATTN_ROW_GROUPS = 2
TM_POST = 256
TM_MOE = 256
TM_COMB = 128
CUM_CHUNK = 512
ROUTER_PAD = 128

U32 = jnp.uint32
N_SLABS = D_MODEL // (2 * LANES)
SLAB_PITCH = 9


def _bf16_bits(v):
    return lax.bitcast_convert_type(v.astype(BF16).astype(F32), U32)


def _store_slabs(ref, base, rows, value):
    for a in range(N_SLABS):
        hi = _bf16_bits(value[:, a * LANES:(a + 1) * LANES])
        lo = _bf16_bits(value[:, (a + N_SLABS) * LANES:(a + N_SLABS + 1) * LANES])
        ref[pl.ds(base + a, rows, stride=SLAB_PITCH), :] = hi | (lo >> 16)
    zeros = jnp.zeros((rows, LANES), U32)
    for a in range(N_SLABS, SLAB_PITCH):
        ref[pl.ds(base + a, rows, stride=SLAB_PITCH), :] = zeros


def _load_slab(ref, base, rows, a):
    w = ref[pl.ds(base + a, rows, stride=SLAB_PITCH), :]
    hi = lax.bitcast_convert_type(w & jnp.uint32(0xFFFF0000), F32)
    lo = lax.bitcast_convert_type(w << 16, F32)
    return hi, lo


def _split_bf16(v):
    hi = v.astype(BF16)
    lo = (v - hi.astype(F32)).astype(BF16)
    return hi, lo


def _cparams(sem, vmem=VMEM_LIMIT):
    return pltpu.CompilerParams(dimension_semantics=sem, vmem_limit_bytes=vmem)


def _adaln_body(c_ref, w_ref, b_ref, o_ref):
    c = c_ref[...]
    cond = c * jax.nn.sigmoid(c)
    o_ref[0] = jnp.sum(w_ref[0] * cond, axis=0, keepdims=True) + b_ref[0]


def _adaln(c_col, w_ada, b_ada):
    depth, d, n = w_ada.shape
    tn = 1024
    return pl.pallas_call(
        _adaln_body,
        grid=(depth, n // tn),
        in_specs=[
            pl.BlockSpec((d, 1), lambda l, j: (0, 0)),
            pl.BlockSpec((1, d, tn), lambda l, j: (l, 0, j)),
            pl.BlockSpec((1, 1, tn), lambda l, j: (l, 0, j)),
        ],
        out_specs=pl.BlockSpec((1, 1, tn), lambda l, j: (l, 0, j)),
        out_shape=jax.ShapeDtypeStruct((depth, 1, n), F32),
        compiler_params=_cparams(("arbitrary", "arbitrary")),
        name="adaln",
    )(c_col, w_ada, b_ada.reshape(depth, 1, n))


def _gelu_tanh(x):
    return 0.5 * x * (1.0 + jnp.tanh(0.7978845608028654 * (x + 0.044715 * (x * x * x))))


def _inproj_body(x_ref, sc_ref, sh_ref, w_ref, wf_ref, z_ref, f_ref, h_scr):
    j = pl.program_id(1)

    @pl.when(j == 0)
    def _():
        h = x_ref[...] * (1.0 + sc_ref[...]) + sh_ref[...]
        h_hi, h_lo = _split_bf16(h)
        h_scr[...] = h_hi
        r1 = jnp.dot(h_hi, wf_ref[...], preferred_element_type=F32)
        r2 = jnp.dot(h_lo, wf_ref[:, :LANES], preferred_element_type=F32)
        f_ref[...] = (r1[:, :LANES] + r1[:, LANES:] + r2)[:, :B_HEADS]

    n_sig = (2 * D_MODEL) // TN_IN
    n_gelu = (2 * A_WIDTH) // TN_IN
    n_sub = TN_IN // TN_SUB

    def project(epilogue):
        for c in range(n_sub):
            cols = slice(c * TN_SUB, (c + 1) * TN_SUB)
            acc = jnp.dot(h_scr[...], w_ref[:, cols], preferred_element_type=F32)
            z_ref[:, cols] = epilogue(acc).astype(BF16)

    @pl.when(j < n_sig)
    def _():
        project(jax.nn.sigmoid)

    @pl.when(jnp.logical_and(j >= n_sig, j < n_sig + n_gelu))
    def _():
        project(_gelu_tanh)

    @pl.when(j >= n_sig + n_gelu)
    def _():
        project(lambda a: a)


def _inproj(x, sc, sh, w_main, w_f):
    s, d = x.shape
    tm = min(TM_IN, s)
    return pl.pallas_call(
        _inproj_body,
        grid=(s // tm, Z_COLS // TN_IN),
        in_specs=[
            pl.BlockSpec((tm, d), lambda i, j: (i, 0)),
            pl.BlockSpec((1, d), lambda i, j: (0, 0)),
            pl.BlockSpec((1, d), lambda i, j: (0, 0)),
            pl.BlockSpec((d, TN_IN), lambda i, j: (0, j)),
            pl.BlockSpec((d, 2 * LANES), lambda i, j: (0, 0)),
        ],
        out_specs=[
            pl.BlockSpec((tm, TN_IN), lambda i, j: (i, j)),
            pl.BlockSpec((tm, B_HEADS), lambda i, j: (i, 0)),
        ],
        out_shape=[
            jax.ShapeDtypeStruct((s, Z_COLS), BF16),
            jax.ShapeDtypeStruct((s, B_HEADS), F32),
        ],
        scratch_shapes=[pltpu.VMEM((tm, d), BF16)],
        compiler_params=_cparams(("arbitrary", "arbitrary")),
        name="inproj",
    )(x, sc, sh, w_main, w_f)


def _cum_body(f_ref, bf_ref, o_ref):
    n_chunks = f_ref.shape[1] // CUM_CHUNK
    row = lax.broadcasted_iota(jnp.int32, (CUM_CHUNK, CUM_CHUNK), 0)
    col = lax.broadcasted_iota(jnp.int32, (CUM_CHUNK, CUM_CHUNK), 1)
    upper = jnp.where(row <= col, 1.0, 0.0).astype(F32)

    def body(i, carry):
        sl = pl.ds(pl.multiple_of(i * CUM_CHUNK, CUM_CHUNK), CUM_CHUNK)
        xf = f_ref[:, sl] + bf_ref[...]
        logf = jnp.minimum(xf, 0.0) - jnp.log1p(jnp.exp(-jnp.abs(xf)))
        cs = jnp.dot(logf, upper, precision=HIGHEST, preferred_element_type=F32) + carry
        o_ref[:, sl] = cs
        return cs[:, CUM_CHUNK - 1:CUM_CHUNK]

    lax.fori_loop(0, n_chunks, body, jnp.zeros((B_HEADS, 1), F32))


def _forget_cumsum(f_t, b_f_col):
    h, s = f_t.shape
    return pl.pallas_call(
        _cum_body,
        out_shape=jax.ShapeDtypeStruct((h, s), F32),
        compiler_params=pltpu.CompilerParams(vmem_limit_bytes=VMEM_LIMIT),
        name="forget_cumsum",
    )(f_t, b_f_col)


def _sgu_body(u_ref, v_ref, g_ref, w_ref, b_ref, o_ref):
    tm = u_ref.shape[0]
    row = lax.broadcasted_iota(jnp.int32, (CHUNK, CHUNK), 0)
    col = lax.broadcasted_iota(jnp.int32, (CHUNK, CHUNK), 1)
    causal = row >= col
    for g in range(A_GROUPS):
        cols = slice(g * CHUNK, (g + 1) * CHUNK)
        w = jnp.where(causal, w_ref[g], 0.0).astype(BF16)
        bias = b_ref[:, g:g + 1]
        gain = g_ref[:, cols]
        for ch in range(tm // CHUNK):
            rows = slice(ch * CHUNK, (ch + 1) * CHUNK)
            v = v_ref[rows, cols].astype(F32)
            mu = jnp.mean(v, axis=-1, keepdims=True)
            vc = v - mu
            var = jnp.mean(vc * vc, axis=-1, keepdims=True)
            vn = vc * lax.rsqrt(var + LN_EPS) * gain
            mixed = jnp.dot(w, vn.astype(BF16), preferred_element_type=F32) + bias
            o_ref[rows, cols] = (u_ref[rows, cols].astype(F32) * mixed).astype(BF16)


def _sgu(z, ln_g, w_s, b_s_t):
    s = z.shape[0]
    tm = min(TM_SGU, s)
    return pl.pallas_call(
        _sgu_body,
        grid=(s // tm,),
        in_specs=[
            pl.BlockSpec((tm, A_WIDTH), lambda i: (i, Z_U // A_WIDTH)),
            pl.BlockSpec((tm, A_WIDTH), lambda i: (i, Z_V // A_WIDTH)),
            pl.BlockSpec((1, A_WIDTH), lambda i: (0, 0)),
            pl.BlockSpec((A_GROUPS, CHUNK, CHUNK), lambda i: (0, 0, 0)),
            pl.BlockSpec((CHUNK, A_GROUPS), lambda i: (0, 0)),
        ],
        out_specs=pl.BlockSpec((tm, A_WIDTH), lambda i: (i, 0)),
        out_shape=jax.ShapeDtypeStruct((s, A_WIDTH), BF16),
        compiler_params=_cparams(("arbitrary",)),
        name="sgu",
    )(z, z, ln_g, w_s, b_s_t)


def _attn_body(lo_ref, q_ref, k_ref, v_ref, ck_ref, o_ref, m_scr, acc_scr, s_even, s_odd):
    h = pl.program_id(0)
    i = pl.program_id(1)
    tq = q_ref.shape[0]
    tk = tq
    scale = B_HEAD_DIM ** -0.5
    q = q_ref[...]
    c_ref = ck_ref[0, :, pl.ds(pl.multiple_of(i * tq, tq), LANES)][:, 0:1]
    ones = jnp.ones((tk, B_HEAD_DIM), BF16)
    lo = lo_ref[h, i]

    m_scr[...] = jnp.full((tq, 1), MASK_VALUE, F32)
    acc_scr[...] = jnp.zeros((tq, 2 * B_HEAD_DIM), F32)

    def key_slice(j):
        return pl.ds(pl.multiple_of(j * tk, tk), tk)

    def raw_scores(j):
        return lax.dot_general(q, k_ref[key_slice(j), :], (((1,), (1,)), ((), ())),
                               preferred_element_type=F32)

    def consume(s_ref, j, masked):
        ks = key_slice(j)
        v_aug = jnp.concatenate([v_ref[ks, :], ones], axis=1)
        bias = c_ref - ck_ref[0, :, ks]
        rows_per_group = tq // ATTN_ROW_GROUPS
        for g in range(ATTN_ROW_GROUPS):
            rows = slice(g * rows_per_group, (g + 1) * rows_per_group)
            s = s_ref[rows, :] * scale + bias
            if masked:
                r = lax.broadcasted_iota(jnp.int32, (rows_per_group, tk), 0) + g * rows_per_group
                c = lax.broadcasted_iota(jnp.int32, (rows_per_group, tk), 1)
                s = jnp.where(r >= c, s, MASK_VALUE)
            m_prev = m_scr[rows, :]
            m_new = jnp.maximum(m_prev, jnp.max(s, axis=1, keepdims=True))
            p = jnp.exp(s - m_new)
            alpha = jnp.exp(m_prev - m_new)
            acc_scr[rows, :] = alpha * acc_scr[rows, :] + jnp.dot(
                p.astype(BF16), v_aug, preferred_element_type=F32)
            m_scr[rows, :] = m_new

    s_even[...] = raw_scores(lo)

    def body(j, carry):
        even_visit = lax.rem(j - lo, 2) == 0

        @pl.when(even_visit)
        def _():
            s_odd[...] = raw_scores(j + 1)
            consume(s_even, j, False)

        @pl.when(jnp.logical_not(even_visit))
        def _():
            s_even[...] = raw_scores(j + 1)
            consume(s_odd, j, False)

        return carry

    lax.fori_loop(lo, i, body, 0)
    last_even = lax.rem(i - lo, 2) == 0

    @pl.when(last_even)
    def _():
        consume(s_even, i, True)

    @pl.when(jnp.logical_not(last_even))
    def _():
        consume(s_odd, i, True)

    acc = acc_scr[...]
    o_ref[...] = (acc[:, :B_HEAD_DIM] / acc[:, B_HEAD_DIM:]).astype(BF16)


def _attention(z, cum_hs, lo):
    s = z.shape[0]
    tq = min(TQ, s)
    grid_spec = pltpu.PrefetchScalarGridSpec(
        num_scalar_prefetch=1,
        grid=(B_HEADS, s // tq),
        in_specs=[
            pl.BlockSpec((tq, B_HEAD_DIM), lambda h, i, lo: (i, Z_Q // B_HEAD_DIM + h)),
            pl.BlockSpec((s, B_HEAD_DIM), lambda h, i, lo: (0, Z_K // B_HEAD_DIM + h)),
            pl.BlockSpec((s, B_HEAD_DIM), lambda h, i, lo: (0, Z_VB // B_HEAD_DIM + h)),
            pl.BlockSpec((1, 1, s), lambda h, i, lo: (h, 0, 0)),
        ],
        out_specs=pl.BlockSpec((tq, B_HEAD_DIM), lambda h, i, lo: (i, h)),
        scratch_shapes=[
            pltpu.VMEM((tq, 1), F32),
            pltpu.VMEM((tq, 2 * B_HEAD_DIM), F32),
            pltpu.VMEM((tq, tq), F32),
            pltpu.VMEM((tq, tq), F32),
        ],
    )
    return pl.pallas_call(
        _attn_body,
        grid_spec=grid_spec,
        out_shape=jax.ShapeDtypeStruct((s, B_WIDTH), BF16),
        compiler_params=_cparams(("arbitrary", "arbitrary")),
        name="fox_attention",
    )(lo, z, z, z, cum_hs.reshape(B_HEADS, 1, s))


def _tile_norm_body(q_ref, k_ref, qo_ref, ko_ref):
    col = lax.broadcasted_iota(jnp.int32, (B_WIDTH, LANES), 0)
    head = lax.broadcasted_iota(jnp.int32, (B_WIDTH, LANES), 1)
    sel = jnp.where(col // B_HEAD_DIM == head, 1.0, 0.0).astype(BF16)
    for src, dst in ((q_ref, qo_ref), (k_ref, ko_ref)):
        zc = src[...]
        sumsq = jnp.dot(zc * zc, sel, preferred_element_type=F32)
        dst[...] = jnp.broadcast_to(jnp.max(sumsq, axis=0, keepdims=True), dst.shape)


def _tile_norms(z):
    s = z.shape[0]
    tq = min(TQ, s)
    nblk = s // tq
    sub = 8
    out = jax.ShapeDtypeStruct((nblk * sub, LANES), F32)
    q_max, k_max = pl.pallas_call(
        _tile_norm_body,
        grid=(nblk,),
        in_specs=[
            pl.BlockSpec((tq, B_WIDTH), lambda i: (i, Z_Q // B_WIDTH)),
            pl.BlockSpec((tq, B_WIDTH), lambda i: (i, Z_K // B_WIDTH)),
        ],
        out_specs=[pl.BlockSpec((sub, LANES), lambda i: (i, 0)),
                   pl.BlockSpec((sub, LANES), lambda i: (i, 0))],
        out_shape=[out, out],
        compiler_params=_cparams(("arbitrary",)),
        name="tile_norms",
    )(z, z)
    pick = lambda t: t.reshape(nblk, sub, LANES)[:, 0, :B_HEADS]
    return pick(q_max), pick(k_max)


def _prune_bounds(q_sumsq, k_sumsq, cum_hs):
    s = cum_hs.shape[1]
    tq = min(TQ, s)
    nblk = s // tq
    scale = B_HEAD_DIM ** -0.5

    qn = jnp.sqrt(q_sumsq).T
    kn = jnp.sqrt(k_sumsq).T
    kpm = lax.cummax(kn, axis=1)
    c_first = cum_hs[:, ::tq]
    c_last = cum_hs[:, tq - 1::tq]
    dot_bound = scale * qn[:, :, None] * (kpm[:, None, :] + kn[:, :, None]) * NORM_SLACK
    bound = dot_bound + c_first[:, :, None] - c_last[:, None, :]
    tile = jnp.arange(nblk, dtype=jnp.int32)
    skip = jnp.logical_and(bound <= -PRUNE_MARGIN, tile[None, None, :] < tile[None, :, None])
    return jnp.sum(skip.astype(jnp.int32), axis=-1)


def _layer_norm(y, g, b):
    mu = jnp.mean(y, axis=-1, keepdims=True)
    yc = y - mu
    var = jnp.mean(yc * yc, axis=-1, keepdims=True)
    return yc * lax.rsqrt(var + LN_EPS) * g + b


def _post_body(ya_ref, yb_ref, ga_ref, gb_ref, x_ref, wa_ref, wb_ref, wo_ref, g1_ref, lng_ref,
               lnb_ref, sc_ref, sh_ref, rw_ref, rb_ref, x1_ref, h2_ref, ti_ref, tw_ref):
    tm = x_ref.shape[0]
    a = jnp.dot(ya_ref[...], wa_ref[...], preferred_element_type=F32)
    b = jnp.dot(yb_ref[...], wb_ref[...], preferred_element_type=F32)
    merged = ga_ref[...].astype(F32) * a + gb_ref[...].astype(F32) * b
    mix = jnp.dot(merged.astype(BF16), wo_ref[...], preferred_element_type=F32)
    x1 = _layer_norm(DEEPNORM_ALPHA * x_ref[...] + (1.0 + g1_ref[...]) * mix,
                     lng_ref[...], lnb_ref[...])
    x1_ref[...] = x1
    h2 = x1 * (1.0 + sc_ref[...]) + sh_ref[...]
    _store_slabs(h2_ref, 0, tm, h2)
    h_hi, h_lo = _split_bf16(h2)
    r1 = jnp.dot(h_hi, rw_ref[...], preferred_element_type=F32)
    r2 = jnp.dot(h_lo, rw_ref[:, :ROUTER_PAD], preferred_element_type=F32)
    logits = r1[:, :ROUTER_PAD] + r1[:, ROUTER_PAD:] + r2 + rb_ref[...]
    lane = lax.broadcasted_iota(jnp.int32, (tm, ROUTER_PAD), 1)
    lane_f = lane.astype(F32)
    neg_inf = float("-inf")
    work = logits
    vals, idxs = [], []
    for _ in range(TOP_K):
        m = jnp.max(work, axis=1, keepdims=True)
        idx = jnp.min(jnp.where(work == m, lane_f, float(ROUTER_PAD)), axis=1, keepdims=True)
        vals.append(m)
        idxs.append(idx)
        work = jnp.where(lane_f == idx, neg_inf, work)
    exps = [jnp.exp(v - vals[0]) for v in vals]
    denom = exps[0] + exps[1] + exps[2] + exps[3]
    ti = jnp.zeros((tm, ROUTER_PAD), F32)
    tw = jnp.zeros((tm, ROUTER_PAD), F32)
    for k in range(TOP_K):
        ti = jnp.where(lane == k, idxs[k], ti)
        tw = jnp.where(lane == k, exps[k] / denom, tw)
    ti_ref[...] = ti.astype(jnp.int32)
    tw_ref[...] = tw


def _post_mixer(ya, yb, z, x, wa, wb, wo, g1, ln_g, ln_b, sc2, sh2, rw_pad, rb_pad):
    s, d = x.shape
    tm = min(TM_POST, s)
    row = lambda i: (i, 0)
    const = lambda i: (0, 0)
    single = pl.Buffered(1)
    return pl.pallas_call(
        _post_body,
        grid=(s // tm,),
        in_specs=[
            pl.BlockSpec((tm, A_WIDTH), row),
            pl.BlockSpec((tm, B_WIDTH), row),
            pl.BlockSpec((tm, d), lambda i: (i, Z_GA // D_MODEL)),
            pl.BlockSpec((tm, d), lambda i: (i, Z_GB // D_MODEL)),
            pl.BlockSpec((tm, d), row),
            pl.BlockSpec((A_WIDTH, d), const, pipeline_mode=single),
            pl.BlockSpec((B_WIDTH, d), const, pipeline_mode=single),
            pl.BlockSpec((d, d), const, pipeline_mode=single),
            pl.BlockSpec((1, d), const),
            pl.BlockSpec((1, d), const),
            pl.BlockSpec((1, d), const),
            pl.BlockSpec((1, d), const),
            pl.BlockSpec((1, d), const),
            pl.BlockSpec((d, 2 * ROUTER_PAD), const),
            pl.BlockSpec((1, ROUTER_PAD), const),
        ],
        out_specs=[
            pl.BlockSpec((tm, d), row),
            pl.BlockSpec((tm * SLAB_PITCH, LANES), row),
            pl.BlockSpec((tm, ROUTER_PAD), row),
            pl.BlockSpec((tm, ROUTER_PAD), row),
        ],
        out_shape=[
            jax.ShapeDtypeStruct((s, d), F32),
            jax.ShapeDtypeStruct((s * SLAB_PITCH, LANES), U32),
            jax.ShapeDtypeStruct((s, ROUTER_PAD), jnp.int32),
            jax.ShapeDtypeStruct((s, ROUTER_PAD), F32),
        ],
        compiler_params=_cparams(("arbitrary",)),
        name="post_mixer",
    )(ya, yb, z, z, x, wa, wb, wo, g1, ln_g, ln_b, sc2, sh2, rw_pad, rb_pad)


def _slab_copy(src_hbm, dst_vmem, src_row, dst_row, sem):
    return pltpu.make_async_copy(src_hbm.at[pl.ds(src_row, N_SLABS)],
                                 dst_vmem.at[pl.ds(dst_row, N_SLABS)], sem)


def _gather_rows(src_hbm, dst_vmem, idx_ref, n, dst_base, dst_row, sem, unrolled):
    if unrolled:
        for r in range(n):
            _slab_copy(src_hbm, dst_vmem, idx_ref[0, 0, r], dst_base + dst_row(r) * SLAB_PITCH, sem).start()
    else:
        def issue(r, carry):
            _slab_copy(src_hbm, dst_vmem, idx_ref[0, 0, r], dst_base + dst_row(r) * SLAB_PITCH, sem).start()
            return carry

        lax.fori_loop(0, n, issue, 0)


def _wait_rows(src_hbm, dst_vmem, n, dst_base, sem):
    def drain(r, carry):
        _slab_copy(src_hbm, dst_vmem, 0, dst_base + r * SLAB_PITCH, sem).wait()
        return carry

    lax.fori_loop(0, n, drain, 0, unroll=8)


def _moe_body(te_ref, nv_ref, rows_ref, rows_next_ref, h_hbm, wg_ref, bg_ref, wu_ref, bu_ref, wd_ref,
              bd_ref, y_ref, xbuf, wg16, wu16, wd16, sem):
    i = pl.program_id(0)
    tm = xbuf.shape[0] // (2 * SLAB_PITCH)
    slot = lax.rem(i, 2)
    slot_rows = tm * SLAB_PITCH

    same_row = lambda r: r

    @pl.when(i == 0)
    def _():
        _gather_rows(h_hbm, xbuf, rows_ref, tm, 0, same_row, sem.at[0], unrolled=False)

    @pl.when(i + 1 < nv_ref[0])
    def _():
        _gather_rows(h_hbm, xbuf, rows_next_ref, tm, (1 - slot) * slot_rows, same_row,
                     sem.at[1 - slot], unrolled=True)

    @pl.when(i < nv_ref[0])
    def _():
        new_expert = jnp.logical_or(i == 0, te_ref[i] != te_ref[jnp.maximum(i - 1, 0)])

        @pl.when(new_expert)
        def _():
            wg16[...] = wg_ref[0].astype(BF16)
            wu16[...] = wu_ref[0].astype(BF16)
            wd16[...] = wd_ref[0].astype(BF16)

        base = slot * slot_rows
        _wait_rows(h_hbm, xbuf, tm, base, sem.at[slot])
        halves = [_load_slab(xbuf, base, tm, a) for a in range(N_SLABS)]
        xb = jnp.concatenate([hi.astype(BF16) for hi, _ in halves] +
                             [lo.astype(BF16) for _, lo in halves], axis=1)
        g = jnp.minimum(jnp.dot(xb, wg16[...], preferred_element_type=F32) + bg_ref[0], SWIGLU_LIMIT)
        u = jnp.clip(jnp.dot(xb, wu16[...], preferred_element_type=F32) + bu_ref[0],
                     -SWIGLU_LIMIT, SWIGLU_LIMIT)
        act = g * jax.nn.sigmoid(SWIGLU_ALPHA * g) * (u + 1.0)
        y = jnp.dot(act.astype(BF16), wd16[...], preferred_element_type=F32) + bd_ref[0]
        _store_slabs(y_ref, 0, tm, y)

    @pl.when(i >= nv_ref[0])
    def _():
        y_ref[...] = jnp.zeros(y_ref.shape, U32)


def _moe_experts(layer, tile_expert, n_valid, row_token, h2, wg, bg, wu, bu, wd, bd):
    d = D_MODEL
    n_tiles = row_token.shape[0]
    tm = row_token.shape[2]
    wmap = lambda i, te, nv: (layer * N_EXPERTS + te[i], 0, 0)
    grid_spec = pltpu.PrefetchScalarGridSpec(
        num_scalar_prefetch=2,
        grid=(n_tiles,),
        in_specs=[
            pl.BlockSpec((1, 1, tm), lambda i, te, nv: (i, 0, 0), memory_space=pltpu.SMEM),
            pl.BlockSpec((1, 1, tm), lambda i, te, nv: (jnp.minimum(i + 1, n_tiles - 1), 0, 0),
                         memory_space=pltpu.SMEM),
            pl.BlockSpec(memory_space=pl.ANY),
            pl.BlockSpec((1, d, D_EXPERT), wmap),
            pl.BlockSpec((1, 1, D_EXPERT), wmap),
            pl.BlockSpec((1, d, D_EXPERT), wmap),
            pl.BlockSpec((1, 1, D_EXPERT), wmap),
            pl.BlockSpec((1, D_EXPERT, d), wmap),
            pl.BlockSpec((1, 1, d), wmap),
        ],
        out_specs=pl.BlockSpec((tm * SLAB_PITCH, LANES), lambda i, te, nv: (i, 0)),
        scratch_shapes=[
            pltpu.VMEM((2 * tm * SLAB_PITCH, LANES), U32),
            pltpu.VMEM((d, D_EXPERT), BF16),
            pltpu.VMEM((d, D_EXPERT), BF16),
            pltpu.VMEM((D_EXPERT, d), BF16),
            pltpu.SemaphoreType.DMA((2,)),
        ],
    )
    return pl.pallas_call(
        _moe_body,
        grid_spec=grid_spec,
        out_shape=jax.ShapeDtypeStruct((n_tiles * tm * SLAB_PITCH, LANES), U32),
        compiler_params=_cparams(("arbitrary",)),
        name="moe_experts",
    )(tile_expert, n_valid, row_token, row_token, h2, wg, bg, wu, bu, wd, bd)


def _combine_body(pos_ref, pos_next_ref, tw_ref, x_ref, g2_ref, lng_ref, lnb_ref, y_hbm, o_ref, ybuf, sem):
    i = pl.program_id(0)
    tm = x_ref.shape[0]
    n_rows = tm * TOP_K
    slot = lax.rem(i, 2)
    slot_rows = n_rows * SLAB_PITCH
    k_major = lambda r: (r % TOP_K) * tm + r // TOP_K

    @pl.when(i == 0)
    def _():
        _gather_rows(y_hbm, ybuf, pos_ref, n_rows, 0, k_major, sem.at[0], unrolled=False)

    @pl.when(i + 1 < pl.num_programs(0))
    def _():
        _gather_rows(y_hbm, ybuf, pos_next_ref, n_rows, (1 - slot) * slot_rows, k_major,
                     sem.at[1 - slot], unrolled=True)

    base = slot * slot_rows
    _wait_rows(y_hbm, ybuf, n_rows, base, sem.at[slot])

    tw = tw_ref[...]
    first, second = [], []
    for a in range(N_SLABS):
        acc_hi = acc_lo = None
        for k in range(TOP_K):
            hi, lo = _load_slab(ybuf, base + k * tm * SLAB_PITCH, tm, a)
            wk = tw[:, k:k + 1]
            acc_hi = wk * hi if acc_hi is None else acc_hi + wk * hi
            acc_lo = wk * lo if acc_lo is None else acc_lo + wk * lo
        first.append(acc_hi)
        second.append(acc_lo)
    ffn = jnp.concatenate(first + second, axis=1)
    o_ref[...] = _layer_norm(DEEPNORM_ALPHA * x_ref[...] + (1.0 + g2_ref[...]) * ffn,
                             lng_ref[...], lnb_ref[...])


def _combine(pos, tw, x1, g2, ln_g, ln_b, y_sorted):
    s, d = x1.shape
    tm = min(TM_COMB, s)
    n_steps = s // tm
    row = lambda i: (i, 0)
    const = lambda i: (0, 0)
    pos3 = pos.reshape(n_steps, 1, tm * TOP_K)
    return pl.pallas_call(
        _combine_body,
        grid=(n_steps,),
        in_specs=[
            pl.BlockSpec((1, 1, tm * TOP_K), lambda i: (i, 0, 0), memory_space=pltpu.SMEM),
            pl.BlockSpec((1, 1, tm * TOP_K), lambda i: (jnp.minimum(i + 1, n_steps - 1), 0, 0),
                         memory_space=pltpu.SMEM),
            pl.BlockSpec((tm, ROUTER_PAD), row),
            pl.BlockSpec((tm, d), row),
            pl.BlockSpec((1, d), const),
            pl.BlockSpec((1, d), const),
            pl.BlockSpec((1, d), const),
            pl.BlockSpec(memory_space=pl.ANY),
        ],
        out_specs=pl.BlockSpec((tm, d), row),
        out_shape=jax.ShapeDtypeStruct((s, d), F32),
        scratch_shapes=[pltpu.VMEM((2 * TOP_K * tm * SLAB_PITCH, LANES), U32),
                        pltpu.SemaphoreType.DMA((2,))],
        compiler_params=_cparams(("arbitrary",)),
        name="moe_combine",
    )(pos3, pos3, tw, x1, g2, ln_g, ln_b, y_sorted)


def _routing_tables(top_i, s):
    tm = TM_MOE
    n_pairs = s * TOP_K
    n_tiles = n_pairs // tm + N_EXPERTS
    flat_e = top_i.reshape(n_pairs)
    onehot = (flat_e[:, None] == jnp.arange(N_EXPERTS, dtype=jnp.int32)[None, :]).astype(jnp.int32)
    csum = jnp.cumsum(onehot, axis=0)
    rank = jnp.sum(csum * onehot, axis=1) - 1
    counts = csum[-1]
    padded = ((counts + tm - 1) // tm) * tm
    ends = jnp.cumsum(padded)
    starts = ends - padded
    pos = jnp.sum(onehot * starts[None, :], axis=1) + rank
    row_token = jnp.zeros((n_tiles * tm,), jnp.int32).at[pos].set(
        (jnp.arange(n_pairs, dtype=jnp.int32) // TOP_K) * SLAB_PITCH)
    tile_start = jnp.arange(n_tiles, dtype=jnp.int32) * tm
    tile_expert = jnp.sum((tile_start[:, None] >= ends[None, :]).astype(jnp.int32), axis=1)
    tile_expert = jnp.minimum(tile_expert, N_EXPERTS - 1).astype(jnp.int32)
    n_valid = (ends[-1] // tm).astype(jnp.int32).reshape(1)
    pos_flat = (pos * SLAB_PITCH).astype(jnp.int32)
    return pos_flat, row_token.reshape(n_tiles, 1, tm), tile_expert, n_valid


def kernel(x, c, w_ada, b_ada, w_in, b_f, sgu_ln_g, spatial_w, spatial_b, w_proj_a, w_proj_b, w_o,
           ln1_g, ln1_b, router_w, router_b, w_gate, b_gate, w_up, b_up, w_down, b_down, ln2_g, ln2_b):
    bsz, s, d = x.shape
    assert bsz == 1
    depth = w_ada.shape[0]
    xs = x.reshape(s, d)
    mod = _adaln(c.reshape(d, 1), w_ada, b_ada)

    n_a = 2 * A_WIDTH
    n_b = 3 * B_WIDTH
    w_main = jnp.concatenate(
        [w_in[:, :, n_a + n_b + B_HEADS:], w_in[:, :, :n_a + n_b]], axis=-1).astype(BF16)

    def split_pad(w):
        hi = w.astype(BF16)
        lo = (w - hi.astype(F32)).astype(BF16)
        pad = ((0, 0), (0, 0), (0, LANES - w.shape[-1]))
        return jnp.concatenate([jnp.pad(hi, pad), jnp.pad(lo, pad)], axis=-1)

    w_f = split_pad(w_in[:, :, n_a + n_b:n_a + n_b + B_HEADS])
    wa16 = w_proj_a.astype(BF16)
    wb16 = w_proj_b.astype(BF16)
    wo16 = w_o.astype(BF16)
    rw_pad = split_pad(router_w)
    rb_pad = jnp.pad(router_b, ((0, 0), (0, ROUTER_PAD - N_EXPERTS)), constant_values=float("-inf"))

    n_all = depth * N_EXPERTS
    wg_all = w_gate.reshape(n_all, d, D_EXPERT)
    wu_all = w_up.reshape(n_all, d, D_EXPERT)
    wd_all = w_down.reshape(n_all, D_EXPERT, d)
    bg_all = b_gate.reshape(n_all, 1, D_EXPERT)
    bu_all = b_up.reshape(n_all, 1, D_EXPERT)
    bd_all = b_down.reshape(n_all, 1, d)

    for l in range(depth):
        sh1, sc1, g1, sh2, sc2, g2 = [mod[l, :, k * d:(k + 1) * d] for k in range(6)]
        z, f_logit = _inproj(xs, sc1, sh1, w_main[l], w_f[l])
        q_sumsq, k_sumsq = _tile_norms(z)
        cum_hs = _forget_cumsum(f_logit.T, b_f[l].reshape(B_HEADS, 1))
        ya = _sgu(z, sgu_ln_g[l].reshape(1, A_WIDTH), spatial_w[l], spatial_b[l].T)
        yb = _attention(z, cum_hs, _prune_bounds(q_sumsq, k_sumsq, cum_hs))
        x1, h2, top_i, top_w = _post_mixer(
            ya, yb, z, xs, wa16[l], wb16[l], wo16[l], g1, ln1_g[l].reshape(1, d),
            ln1_b[l].reshape(1, d), sc2, sh2, rw_pad[l], rb_pad[l].reshape(1, ROUTER_PAD))
        pos, row_token, tile_expert, n_valid = _routing_tables(top_i[:, :TOP_K], s)
        y_sorted = _moe_experts(l, tile_expert, n_valid, row_token, h2, wg_all, bg_all, wu_all, bu_all,
                                wd_all, bd_all)
        xs = _combine(pos, top_w, x1, g2, ln2_g[l].reshape(1, d), ln2_b[l].reshape(1, d), y_sorted)
    return xs.reshape(bsz, s, d)
```

```python
import functools

import jax
import jax.numpy as jnp
from jax import lax
from jax.experimental import pallas as pl
from jax.experimental.pallas import tpu as pltpu

F32 = jnp.float32
BF16 = jnp.bfloat16
HIGHEST = lax.Precision.HIGHEST

D_MODEL = 2048
DEPTH = 4
CHUNK = 128
A_GROUPS = 8
A_WIDTH = 1024
B_HEADS = 8
B_HEAD_DIM = 128
B_WIDTH = 1024
N_EXPERTS = 32
TOP_K = 4
D_EXPERT = 512
SWIGLU_LIMIT = 7.0
SWIGLU_ALPHA = 1.702
DEEPNORM_ALPHA = (2.0 * DEPTH) ** 0.25
LN_EPS = 1e-5
MASK_VALUE = -1e30

LANES = 128
VMEM_LIMIT = 56 * 1024 * 1024

Z_COLS = 2 * D_MODEL + 2 * A_WIDTH + 3 * B_WIDTH
Z_GA, Z_GB, Z_U, Z_V, Z_Q, Z_K, Z_VB = 0, 2048, 4096, 5120, 6144, 7168, 8192

TN_IN = 1024
TN_SUB = 256
TM_IN = 1024
TM_SGU = 256
TQ = 512
PRUNE_MARGIN = 110.0
NORM_SLACK = 1.01
ATTN_ROW_GROUPS = 2
TM_POST = 256
TM_MOE = 256
TM_COMB = 128
CUM_CHUNK = 512
ROUTER_PAD = 128

U32 = jnp.uint32
N_SLABS = D_MODEL // (2 * LANES)
SLAB_PITCH = 9


def _bf16_bits(v):
    return lax.bitcast_convert_type(v.astype(BF16).astype(F32), U32)


def _store_slabs(ref, base, rows, value):
    for a in range(N_SLABS):
        hi = _bf16_bits(value[:, a * LANES:(a + 1) * LANES])
        lo = _bf16_bits(value[:, (a + N_SLABS) * LANES:(a + N_SLABS + 1) * LANES])
        ref[pl.ds(base + a, rows, stride=SLAB_PITCH), :] = hi | (lo >> 16)
    zeros = jnp.zeros((rows, LANES), U32)
    for a in range(N_SLABS, SLAB_PITCH):
        ref[pl.ds(base + a, rows, stride=SLAB_PITCH), :] = zeros


def _load_slab(ref, base, rows, a):
    w = ref[pl.ds(base + a, rows, stride=SLAB_PITCH), :]
    hi = lax.bitcast_convert_type(w & jnp.uint32(0xFFFF0000), F32)
    lo = lax.bitcast_convert_type(w << 16, F32)
    return hi, lo


def _split_bf16(v):
    hi = v.astype(BF16)
    lo = (v - hi.astype(F32)).astype(BF16)
    return hi, lo


def _cparams(sem, vmem=VMEM_LIMIT):
    return pltpu.CompilerParams(dimension_semantics=sem, vmem_limit_bytes=vmem)


def _adaln_body(c_ref, w_ref, b_ref, o_ref):
    c = c_ref[...]
    cond = c * jax.nn.sigmoid(c)
    o_ref[0] = jnp.sum(w_ref[0] * cond, axis=0, keepdims=True) + b_ref[0]


def _adaln(c_col, w_ada, b_ada):
    depth, d, n = w_ada.shape
    tn = 1024
    return pl.pallas_call(
        _adaln_body,
        grid=(depth, n // tn),
        in_specs=[
            pl.BlockSpec((d, 1), lambda l, j: (0, 0)),
            pl.BlockSpec((1, d, tn), lambda l, j: (l, 0, j)),
            pl.BlockSpec((1, 1, tn), lambda l, j: (l, 0, j)),
        ],
        out_specs=pl.BlockSpec((1, 1, tn), lambda l, j: (l, 0, j)),
        out_shape=jax.ShapeDtypeStruct((depth, 1, n), F32),
        compiler_params=_cparams(("arbitrary", "arbitrary")),
        name="adaln",
    )(c_col, w_ada, b_ada.reshape(depth, 1, n))


def _gelu_tanh(x):
    return 0.5 * x * (1.0 + jnp.tanh(0.7978845608028654 * (x + 0.044715 * (x * x * x))))


def _inproj_body(x_ref, sc_ref, sh_ref, w_ref, wf_ref, z_ref, f_ref, h_scr):
    j = pl.program_id(1)

    @pl.when(j == 0)
    def _():
        h = x_ref[...] * (1.0 + sc_ref[...]) + sh_ref[...]
        h_hi, h_lo = _split_bf16(h)
        h_scr[...] = h_hi
        r1 = jnp.dot(h_hi, wf_ref[...], preferred_element_type=F32)
        r2 = jnp.dot(h_lo, wf_ref[:, :LANES], preferred_element_type=F32)
        f_ref[...] = (r1[:, :LANES] + r1[:, LANES:] + r2)[:, :B_HEADS]

    n_sig = (2 * D_MODEL) // TN_IN
    n_gelu = (2 * A_WIDTH) // TN_IN
    n_sub = TN_IN // TN_SUB

    def project(epilogue):
        for c in range(n_sub):
            cols = slice(c * TN_SUB, (c + 1) * TN_SUB)
            acc = jnp.dot(h_scr[...], w_ref[:, cols], preferred_element_type=F32)
            z_ref[:, cols] = epilogue(acc).astype(BF16)

    @pl.when(j < n_sig)
    def _():
        project(jax.nn.sigmoid)

    @pl.when(jnp.logical_and(j >= n_sig, j < n_sig + n_gelu))
    def _():
        project(_gelu_tanh)

    @pl.when(j >= n_sig + n_gelu)
    def _():
        project(lambda a: a)


def _inproj(x, sc, sh, w_main, w_f):
    s, d = x.shape
    tm = min(TM_IN, s)
    return pl.pallas_call(
        _inproj_body,
        grid=(s // tm, Z_COLS // TN_IN),
        in_specs=[
            pl.BlockSpec((tm, d), lambda i, j: (i, 0)),
            pl.BlockSpec((1, d), lambda i, j: (0, 0)),
            pl.BlockSpec((1, d), lambda i, j: (0, 0)),
            pl.BlockSpec((d, TN_IN), lambda i, j: (0, j)),
            pl.BlockSpec((d, 2 * LANES), lambda i, j: (0, 0)),
        ],
        out_specs=[
            pl.BlockSpec((tm, TN_IN), lambda i, j: (i, j)),
            pl.BlockSpec((tm, B_HEADS), lambda i, j: (i, 0)),
        ],
        out_shape=[
            jax.ShapeDtypeStruct((s, Z_COLS), BF16),
            jax.ShapeDtypeStruct((s, B_HEADS), F32),
        ],
        scratch_shapes=[pltpu.VMEM((tm, d), BF16)],
        compiler_params=_cparams(("arbitrary", "arbitrary")),
        name="inproj",
    )(x, sc, sh, w_main, w_f)


def _cum_body(f_ref, bf_ref, o_ref):
    n_chunks = f_ref.shape[1] // CUM_CHUNK
    row = lax.broadcasted_iota(jnp.int32, (CUM_CHUNK, CUM_CHUNK), 0)
    col = lax.broadcasted_iota(jnp.int32, (CUM_CHUNK, CUM_CHUNK), 1)
    upper = jnp.where(row <= col, 1.0, 0.0).astype(F32)

    def body(i, carry):
        sl = pl.ds(pl.multiple_of(i * CUM_CHUNK, CUM_CHUNK), CUM_CHUNK)
        xf = f_ref[:, sl] + bf_ref[...]
        logf = jnp.minimum(xf, 0.0) - jnp.log1p(jnp.exp(-jnp.abs(xf)))
        cs = jnp.dot(logf, upper, precision=HIGHEST, preferred_element_type=F32) + carry
        o_ref[:, sl] = cs
        return cs[:, CUM_CHUNK - 1:CUM_CHUNK]

    lax.fori_loop(0, n_chunks, body, jnp.zeros((B_HEADS, 1), F32))


def _forget_cumsum(f_t, b_f_col):
    h, s = f_t.shape
    return pl.pallas_call(
        _cum_body,
        out_shape=jax.ShapeDtypeStruct((h, s), F32),
        compiler_params=pltpu.CompilerParams(vmem_limit_bytes=VMEM_LIMIT),
        name="forget_cumsum",
    )(f_t, b_f_col)


def _sgu_body(u_ref, v_ref, g_ref, w_ref, b_ref, o_ref):
    tm = u_ref.shape[0]
    row = lax.broadcasted_iota(jnp.int32, (CHUNK, CHUNK), 0)
    col = lax.broadcasted_iota(jnp.int32, (CHUNK, CHUNK), 1)
    causal = row >= col
    for g in range(A_GROUPS):
        cols = slice(g * CHUNK, (g + 1) * CHUNK)
        w = jnp.where(causal, w_ref[g], 0.0).astype(BF16)
        bias = b_ref[:, g:g + 1]
        gain = g_ref[:, cols]
        for ch in range(tm // CHUNK):
            rows = slice(ch * CHUNK, (ch + 1) * CHUNK)
            v = v_ref[rows, cols].astype(F32)
            mu = jnp.mean(v, axis=-1, keepdims=True)
            vc = v - mu
            var = jnp.mean(vc * vc, axis=-1, keepdims=True)
            vn = vc * lax.rsqrt(var + LN_EPS) * gain
            mixed = jnp.dot(w, vn.astype(BF16), preferred_element_type=F32) + bias
            o_ref[rows, cols] = (u_ref[rows, cols].astype(F32) * mixed).astype(BF16)


def _sgu(z, ln_g, w_s, b_s_t):
    s = z.shape[0]
    tm = min(TM_SGU, s)
    return pl.pallas_call(
        _sgu_body,
        grid=(s // tm,),
        in_specs=[
            pl.BlockSpec((tm, A_WIDTH), lambda i: (i, Z_U // A_WIDTH)),
            pl.BlockSpec((tm, A_WIDTH), lambda i: (i, Z_V // A_WIDTH)),
            pl.BlockSpec((1, A_WIDTH), lambda i: (0, 0)),
            pl.BlockSpec((A_GROUPS, CHUNK, CHUNK), lambda i: (0, 0, 0)),
            pl.BlockSpec((CHUNK, A_GROUPS), lambda i: (0, 0)),
        ],
        out_specs=pl.BlockSpec((tm, A_WIDTH), lambda i: (i, 0)),
        out_shape=jax.ShapeDtypeStruct((s, A_WIDTH), BF16),
        compiler_params=_cparams(("arbitrary",)),
        name="sgu",
    )(z, z, ln_g, w_s, b_s_t)


def _attn_body(lo_ref, q_ref, k_ref, v_ref, ck_ref, o_ref, m_scr, acc_scr, s_even, s_odd):
    h = pl.program_id(0)
    i = pl.program_id(1)
    tq = q_ref.shape[0]
    tk = tq
    scale = B_HEAD_DIM ** -0.5
    q = q_ref[...]
    c_ref = ck_ref[0, :, pl.ds(pl.multiple_of(i * tq, tq), LANES)][:, 0:1]
    ones = jnp.ones((tk, B_HEAD_DIM), BF16)
    lo = lo_ref[h, i]

    m_scr[...] = jnp.full((tq, 1), MASK_VALUE, F32)
    acc_scr[...] = jnp.zeros((tq, 2 * B_HEAD_DIM), F32)

    def key_slice(j):
        return pl.ds(pl.multiple_of(j * tk, tk), tk)

    def raw_scores(j):
        return lax.dot_general(q, k_ref[key_slice(j), :], (((1,), (1,)), ((), ())),
                               preferred_element_type=F32)

    def consume(s_ref, j, masked):
        ks = key_slice(j)
        v_aug = jnp.concatenate([v_ref[ks, :], ones], axis=1)
        bias = c_ref - ck_ref[0, :, ks]
        rows_per_group = tq // ATTN_ROW_GROUPS
        for g in range(ATTN_ROW_GROUPS):
            rows = slice(g * rows_per_group, (g + 1) * rows_per_group)
            s = s_ref[rows, :] * scale + bias
            if masked:
                r = lax.broadcasted_iota(jnp.int32, (rows_per_group, tk), 0) + g * rows_per_group
                c = lax.broadcasted_iota(jnp.int32, (rows_per_group, tk), 1)
                s = jnp.where(r >= c, s, MASK_VALUE)
            m_prev = m_scr[rows, :]
            m_new = jnp.maximum(m_prev, jnp.max(s, axis=1, keepdims=True))
            p = jnp.exp(s - m_new)
            alpha = jnp.exp(m_prev - m_new)
            acc_scr[rows, :] = alpha * acc_scr[rows, :] + jnp.dot(
                p.astype(BF16), v_aug, preferred_element_type=F32)
            m_scr[rows, :] = m_new

    s_even[...] = raw_scores(lo)

    def body(j, carry):
        even_visit = lax.rem(j - lo, 2) == 0

        @pl.when(even_visit)
        def _():
            s_odd[...] = raw_scores(j + 1)
            consume(s_even, j, False)

        @pl.when(jnp.logical_not(even_visit))
        def _():
            s_even[...] = raw_scores(j + 1)
            consume(s_odd, j, False)

        return carry

    lax.fori_loop(lo, i, body, 0)
    last_even = lax.rem(i - lo, 2) == 0

    @pl.when(last_even)
    def _():
        consume(s_even, i, True)

    @pl.when(jnp.logical_not(last_even))
    def _():
        consume(s_odd, i, True)

    acc = acc_scr[...]
    o_ref[...] = (acc[:, :B_HEAD_DIM] / acc[:, B_HEAD_DIM:]).astype(BF16)


def _attention(z, cum_hs, lo):
    s = z.shape[0]
    tq = min(TQ, s)
    grid_spec = pltpu.PrefetchScalarGridSpec(
        num_scalar_prefetch=1,
        grid=(B_HEADS, s // tq),
        in_specs=[
            pl.BlockSpec((tq, B_HEAD_DIM), lambda h, i, lo: (i, Z_Q // B_HEAD_DIM + h)),
            pl.BlockSpec((s, B_HEAD_DIM), lambda h, i, lo: (0, Z_K // B_HEAD_DIM + h)),
            pl.BlockSpec((s, B_HEAD_DIM), lambda h, i, lo: (0, Z_VB // B_HEAD_DIM + h)),
            pl.BlockSpec((1, 1, s), lambda h, i, lo: (h, 0, 0)),
        ],
        out_specs=pl.BlockSpec((tq, B_HEAD_DIM), lambda h, i, lo: (i, h)),
        scratch_shapes=[
            pltpu.VMEM((tq, 1), F32),
            pltpu.VMEM((tq, 2 * B_HEAD_DIM), F32),
            pltpu.VMEM((tq, tq), F32),
            pltpu.VMEM((tq, tq), F32),
        ],
    )
    return pl.pallas_call(
        _attn_body,
        grid_spec=grid_spec,
        out_shape=jax.ShapeDtypeStruct((s, B_WIDTH), BF16),
        compiler_params=_cparams(("arbitrary", "arbitrary")),
        name="fox_attention",
    )(lo, z, z, z, cum_hs.reshape(B_HEADS, 1, s))


def _tile_norm_body(q_ref, k_ref, qo_ref, ko_ref):
    col = lax.broadcasted_iota(jnp.int32, (B_WIDTH, LANES), 0)
    head = lax.broadcasted_iota(jnp.int32, (B_WIDTH, LANES), 1)
    sel = jnp.where(col // B_HEAD_DIM == head, 1.0, 0.0).astype(BF16)
    for src, dst in ((q_ref, qo_ref), (k_ref, ko_ref)):
        zc = src[...]
        sumsq = jnp.dot(zc * zc, sel, preferred_element_type=F32)
        dst[...] = jnp.broadcast_to(jnp.max(sumsq, axis=0, keepdims=True), dst.shape)


def _tile_norms(z):
    s = z.shape[0]
    tq = min(TQ, s)
    nblk = s // tq
    sub = 8
    out = jax.ShapeDtypeStruct((nblk * sub, LANES), F32)
    q_max, k_max = pl.pallas_call(
        _tile_norm_body,
        grid=(nblk,),
        in_specs=[
            pl.BlockSpec((tq, B_WIDTH), lambda i: (i, Z_Q // B_WIDTH)),
            pl.BlockSpec((tq, B_WIDTH), lambda i: (i, Z_K // B_WIDTH)),
        ],
        out_specs=[pl.BlockSpec((sub, LANES), lambda i: (i, 0)),
                   pl.BlockSpec((sub, LANES), lambda i: (i, 0))],
        out_shape=[out, out],
        compiler_params=_cparams(("arbitrary",)),
        name="tile_norms",
    )(z, z)
    pick = lambda t: t.reshape(nblk, sub, LANES)[:, 0, :B_HEADS]
    return pick(q_max), pick(k_max)


def _prune_bounds(q_sumsq, k_sumsq, cum_hs):
    s = cum_hs.shape[1]
    tq = min(TQ, s)
    nblk = s // tq
    scale = B_HEAD_DIM ** -0.5

    qn = jnp.sqrt(q_sumsq).T
    kn = jnp.sqrt(k_sumsq).T
    kpm = lax.cummax(kn, axis=1)
    c_first = cum_hs[:, ::tq]
    c_last = cum_hs[:, tq - 1::tq]
    dot_bound = scale * qn[:, :, None] * (kpm[:, None, :] + kn[:, :, None]) * NORM_SLACK
    bound = dot_bound + c_first[:, :, None] - c_last[:, None, :]
    tile = jnp.arange(nblk, dtype=jnp.int32)
    skip = jnp.logical_and(bound <= -PRUNE_MARGIN, tile[None, None, :] < tile[None, :, None])
    return jnp.sum(skip.astype(jnp.int32), axis=-1)


def _layer_norm(y, g, b):
    mu = jnp.mean(y, axis=-1, keepdims=True)
    yc = y - mu
    var = jnp.mean(yc * yc, axis=-1, keepdims=True)
    return yc * lax.rsqrt(var + LN_EPS) * g + b


def _post_body(ya_ref, yb_ref, ga_ref, gb_ref, x_ref, wa_ref, wb_ref, wo_ref, g1_ref, lng_ref,
               lnb_ref, sc_ref, sh_ref, rw_ref, rb_ref, x1_ref, h2_ref, ti_ref, tw_ref):
    tm = x_ref.shape[0]
    a = jnp.dot(ya_ref[...], wa_ref[...], preferred_element_type=F32)
    b = jnp.dot(yb_ref[...], wb_ref[...], preferred_element_type=F32)
    merged = ga_ref[...].astype(F32) * a + gb_ref[...].astype(F32) * b
    mix = jnp.dot(merged.astype(BF16), wo_ref[...], preferred_element_type=F32)
    x1 = _layer_norm(DEEPNORM_ALPHA * x_ref[...] + (1.0 + g1_ref[...]) * mix,
                     lng_ref[...], lnb_ref[...])
    x1_ref[...] = x1
    h2 = x1 * (1.0 + sc_ref[...]) + sh_ref[...]
    _store_slabs(h2_ref, 0, tm, h2)
    h_hi, h_lo = _split_bf16(h2)
    r1 = jnp.dot(h_hi, rw_ref[...], preferred_element_type=F32)
    r2 = jnp.dot(h_lo, rw_ref[:, :ROUTER_PAD], preferred_element_type=F32)
    logits = r1[:, :ROUTER_PAD] + r1[:, ROUTER_PAD:] + r2 + rb_ref[...]
    lane = lax.broadcasted_iota(jnp.int32, (tm, ROUTER_PAD), 1)
    lane_f = lane.astype(F32)
    neg_inf = float("-inf")
    work = logits
    vals, idxs = [], []
    for _ in range(TOP_K):
        m = jnp.max(work, axis=1, keepdims=True)
        idx = jnp.min(jnp.where(work == m, lane_f, float(ROUTER_PAD)), axis=1, keepdims=True)
        vals.append(m)
        idxs.append(idx)
        work = jnp.where(lane_f == idx, neg_inf, work)
    exps = [jnp.exp(v - vals[0]) for v in vals]
    denom = exps[0] + exps[1] + exps[2] + exps[3]
    ti = jnp.zeros((tm, ROUTER_PAD), F32)
    tw = jnp.zeros((tm, ROUTER_PAD), F32)
    for k in range(TOP_K):
        ti = jnp.where(lane == k, idxs[k], ti)
        tw = jnp.where(lane == k, exps[k] / denom, tw)
    ti_ref[...] = ti.astype(jnp.int32)
    tw_ref[...] = tw


def _post_mixer(ya, yb, z, x, wa, wb, wo, g1, ln_g, ln_b, sc2, sh2, rw_pad, rb_pad):
    s, d = x.shape
    tm = min(TM_POST, s)
    row = lambda i: (i, 0)
    const = lambda i: (0, 0)
    single = pl.Buffered(1)
    return pl.pallas_call(
        _post_body,
        grid=(s // tm,),
        in_specs=[
            pl.BlockSpec((tm, A_WIDTH), row),
            pl.BlockSpec((tm, B_WIDTH), row),
            pl.BlockSpec((tm, d), lambda i: (i, Z_GA // D_MODEL)),
            pl.BlockSpec((tm, d), lambda i: (i, Z_GB // D_MODEL)),
            pl.BlockSpec((tm, d), row),
            pl.BlockSpec((A_WIDTH, d), const, pipeline_mode=single),
            pl.BlockSpec((B_WIDTH, d), const, pipeline_mode=single),
            pl.BlockSpec((d, d), const, pipeline_mode=single),
            pl.BlockSpec((1, d), const),
            pl.BlockSpec((1, d), const),
            pl.BlockSpec((1, d), const),
            pl.BlockSpec((1, d), const),
            pl.BlockSpec((1, d), const),
            pl.BlockSpec((d, 2 * ROUTER_PAD), const),
            pl.BlockSpec((1, ROUTER_PAD), const),
        ],
        out_specs=[
            pl.BlockSpec((tm, d), row),
            pl.BlockSpec((tm * SLAB_PITCH, LANES), row),
            pl.BlockSpec((tm, ROUTER_PAD), row),
            pl.BlockSpec((tm, ROUTER_PAD), row),
        ],
        out_shape=[
            jax.ShapeDtypeStruct((s, d), F32),
            jax.ShapeDtypeStruct((s * SLAB_PITCH, LANES), U32),
            jax.ShapeDtypeStruct((s, ROUTER_PAD), jnp.int32),
            jax.ShapeDtypeStruct((s, ROUTER_PAD), F32),
        ],
        compiler_params=_cparams(("arbitrary",)),
        name="post_mixer",
    )(ya, yb, z, z, x, wa, wb, wo, g1, ln_g, ln_b, sc2, sh2, rw_pad, rb_pad)


def _slab_copy(src_hbm, dst_vmem, src_row, dst_row, sem):
    return pltpu.make_async_copy(src_hbm.at[pl.ds(src_row, N_SLABS)],
                                 dst_vmem.at[pl.ds(dst_row, N_SLABS)], sem)


def _gather_rows(src_hbm, dst_vmem, idx_ref, first, stop, dst_base, dst_row, sem, unrolled):
    if unrolled:
        for r in range(first, stop):
            _slab_copy(src_hbm, dst_vmem, idx_ref[0, 0, r], dst_base + dst_row(r) * SLAB_PITCH,
                       sem).start(priority=r % 2)
    else:
        def issue(r, carry):
            _slab_copy(src_hbm, dst_vmem, idx_ref[0, 0, r], dst_base + dst_row(r) * SLAB_PITCH, sem).start()
            return carry

        lax.fori_loop(first, stop, issue, 0)


def _wait_rows(src_hbm, dst_vmem, n, dst_base, sem):
    def drain(r, carry):
        _slab_copy(src_hbm, dst_vmem, 0, dst_base + r * SLAB_PITCH, sem).wait()
        return carry

    lax.fori_loop(0, n, drain, 0, unroll=8)


def _moe_body(te_ref, nv_ref, rows_ref, rows_next_ref, h_hbm, wg_ref, bg_ref, wu_ref, bu_ref, wd_ref,
              bd_ref, y_ref, xbuf, wg16, wu16, wd16, sem):
    i = pl.program_id(0)
    tm = xbuf.shape[0] // (2 * SLAB_PITCH)
    slot = lax.rem(i, 2)
    slot_rows = tm * SLAB_PITCH

    same_row = lambda r: r

    has_next = i + 1 < nv_ref[0]

    def gather_next(first, stop):
        @pl.when(has_next)
        def _():
            _gather_rows(h_hbm, xbuf, rows_next_ref, first, stop, (1 - slot) * slot_rows, same_row,
                         sem.at[1 - slot], unrolled=True)

    @pl.when(i == 0)
    def _():
        _gather_rows(h_hbm, xbuf, rows_ref, 0, tm, 0, same_row, sem.at[0], unrolled=False)

    gather_next(0, tm // 2)

    @pl.when(i < nv_ref[0])
    def _():
        new_expert = jnp.logical_or(i == 0, te_ref[i] != te_ref[jnp.maximum(i - 1, 0)])

        @pl.when(new_expert)
        def _():
            wg16[...] = wg_ref[0].astype(BF16)
            wu16[...] = wu_ref[0].astype(BF16)
            wd16[...] = wd_ref[0].astype(BF16)

        base = slot * slot_rows
        _wait_rows(h_hbm, xbuf, tm, base, sem.at[slot])
        halves = [_load_slab(xbuf, base, tm, a) for a in range(N_SLABS)]
        xb = jnp.concatenate([hi.astype(BF16) for hi, _ in halves] +
                             [lo.astype(BF16) for _, lo in halves], axis=1)
        g = jnp.minimum(jnp.dot(xb, wg16[...], preferred_element_type=F32) + bg_ref[0], SWIGLU_LIMIT)
        u = jnp.clip(jnp.dot(xb, wu16[...], preferred_element_type=F32) + bu_ref[0],
                     -SWIGLU_LIMIT, SWIGLU_LIMIT)
        act = (g * jax.nn.sigmoid(SWIGLU_ALPHA * g) * (u + 1.0)).astype(BF16)
        gather_next(tm // 2, tm)
        y = jnp.dot(act, wd16[...], preferred_element_type=F32) + bd_ref[0]
        _store_slabs(y_ref, 0, tm, y)

    @pl.when(i >= nv_ref[0])
    def _():
        y_ref[...] = jnp.zeros(y_ref.shape, U32)


def _moe_experts(layer, tile_expert, n_valid, row_token, h2, wg, bg, wu, bu, wd, bd):
    d = D_MODEL
    n_tiles = row_token.shape[0]
    tm = row_token.shape[2]
    wmap = lambda i, te, nv: (layer * N_EXPERTS + te[i], 0, 0)
    grid_spec = pltpu.PrefetchScalarGridSpec(
        num_scalar_prefetch=2,
        grid=(n_tiles,),
        in_specs=[
            pl.BlockSpec((1, 1, tm), lambda i, te, nv: (i, 0, 0), memory_space=pltpu.SMEM),
            pl.BlockSpec((1, 1, tm), lambda i, te, nv: (jnp.minimum(i + 1, n_tiles - 1), 0, 0),
                         memory_space=pltpu.SMEM),
            pl.BlockSpec(memory_space=pl.ANY),
            pl.BlockSpec((1, d, D_EXPERT), wmap),
            pl.BlockSpec((1, 1, D_EXPERT), wmap),
            pl.BlockSpec((1, d, D_EXPERT), wmap),
            pl.BlockSpec((1, 1, D_EXPERT), wmap),
            pl.BlockSpec((1, D_EXPERT, d), wmap),
            pl.BlockSpec((1, 1, d), wmap),
        ],
        out_specs=pl.BlockSpec((tm * SLAB_PITCH, LANES), lambda i, te, nv: (i, 0)),
        scratch_shapes=[
            pltpu.VMEM((2 * tm * SLAB_PITCH, LANES), U32),
            pltpu.VMEM((d, D_EXPERT), BF16),
            pltpu.VMEM((d, D_EXPERT), BF16),
            pltpu.VMEM((D_EXPERT, d), BF16),
            pltpu.SemaphoreType.DMA((2,)),
        ],
    )
    return pl.pallas_call(
        _moe_body,
        grid_spec=grid_spec,
        out_shape=jax.ShapeDtypeStruct((n_tiles * tm * SLAB_PITCH, LANES), U32),
        compiler_params=_cparams(("arbitrary",)),
        name="moe_experts",
    )(tile_expert, n_valid, row_token, row_token, h2, wg, bg, wu, bu, wd, bd)


def _combine_body(pos_ref, pos_next_ref, tw_ref, x_ref, g2_ref, lng_ref, lnb_ref, y_hbm, o_ref, ybuf, sem):
    i = pl.program_id(0)
    tm = x_ref.shape[0]
    n_rows = tm * TOP_K
    slot = lax.rem(i, 2)
    slot_rows = n_rows * SLAB_PITCH
    k_major = lambda r: (r % TOP_K) * tm + r // TOP_K

    @pl.when(i == 0)
    def _():
        _gather_rows(y_hbm, ybuf, pos_ref, 0, n_rows, 0, k_major, sem.at[0], unrolled=False)

    @pl.when(i + 1 < pl.num_programs(0))
    def _():
        _gather_rows(y_hbm, ybuf, pos_next_ref, 0, n_rows, (1 - slot) * slot_rows, k_major,
                     sem.at[1 - slot], unrolled=True)

    base = slot * slot_rows
    _wait_rows(y_hbm, ybuf, n_rows, base, sem.at[slot])

    tw = tw_ref[...]
    first, second = [], []
    for a in range(N_SLABS):
        acc_hi = acc_lo = None
        for k in range(TOP_K):
            hi, lo = _load_slab(ybuf, base + k * tm * SLAB_PITCH, tm, a)
            wk = tw[:, k:k + 1]
            acc_hi = wk * hi if acc_hi is None else acc_hi + wk * hi
            acc_lo = wk * lo if acc_lo is None else acc_lo + wk * lo
        first.append(acc_hi)
        second.append(acc_lo)
    ffn = jnp.concatenate(first + second, axis=1)
    o_ref[...] = _layer_norm(DEEPNORM_ALPHA * x_ref[...] + (1.0 + g2_ref[...]) * ffn,
                             lng_ref[...], lnb_ref[...])


def _combine(pos, tw, x1, g2, ln_g, ln_b, y_sorted):
    s, d = x1.shape
    tm = min(TM_COMB, s)
    n_steps = s // tm
    row = lambda i: (i, 0)
    const = lambda i: (0, 0)
    pos3 = pos.reshape(n_steps, 1, tm * TOP_K)
    return pl.pallas_call(
        _combine_body,
        grid=(n_steps,),
        in_specs=[
            pl.BlockSpec((1, 1, tm * TOP_K), lambda i: (i, 0, 0), memory_space=pltpu.SMEM),
            pl.BlockSpec((1, 1, tm * TOP_K), lambda i: (jnp.minimum(i + 1, n_steps - 1), 0, 0),
                         memory_space=pltpu.SMEM),
            pl.BlockSpec((tm, ROUTER_PAD), row),
            pl.BlockSpec((tm, d), row),
            pl.BlockSpec((1, d), const),
            pl.BlockSpec((1, d), const),
            pl.BlockSpec((1, d), const),
            pl.BlockSpec(memory_space=pl.ANY),
        ],
        out_specs=pl.BlockSpec((tm, d), row),
        out_shape=jax.ShapeDtypeStruct((s, d), F32),
        scratch_shapes=[pltpu.VMEM((2 * TOP_K * tm * SLAB_PITCH, LANES), U32),
                        pltpu.SemaphoreType.DMA((2,))],
        compiler_params=_cparams(("arbitrary",)),
        name="moe_combine",
    )(pos3, pos3, tw, x1, g2, ln_g, ln_b, y_sorted)


def _routing_tables(top_i, s):
    tm = TM_MOE
    n_pairs = s * TOP_K
    n_tiles = n_pairs // tm + N_EXPERTS
    flat_e = top_i.reshape(n_pairs)
    onehot = (flat_e[:, None] == jnp.arange(N_EXPERTS, dtype=jnp.int32)[None, :]).astype(jnp.int32)
    csum = jnp.cumsum(onehot, axis=0)
    rank = jnp.sum(csum * onehot, axis=1) - 1
    counts = csum[-1]
    padded = ((counts + tm - 1) // tm) * tm
    ends = jnp.cumsum(padded)
    starts = ends - padded
    pos = jnp.sum(onehot * starts[None, :], axis=1) + rank
    row_token = jnp.zeros((n_tiles * tm,), jnp.int32).at[pos].set(
        (jnp.arange(n_pairs, dtype=jnp.int32) // TOP_K) * SLAB_PITCH)
    tile_start = jnp.arange(n_tiles, dtype=jnp.int32) * tm
    tile_expert = jnp.sum((tile_start[:, None] >= ends[None, :]).astype(jnp.int32), axis=1)
    tile_expert = jnp.minimum(tile_expert, N_EXPERTS - 1).astype(jnp.int32)
    n_valid = (ends[-1] // tm).astype(jnp.int32).reshape(1)
    pos_flat = (pos * SLAB_PITCH).astype(jnp.int32)
    return pos_flat, row_token.reshape(n_tiles, 1, tm), tile_expert, n_valid


def kernel(x, c, w_ada, b_ada, w_in, b_f, sgu_ln_g, spatial_w, spatial_b, w_proj_a, w_proj_b, w_o,
           ln1_g, ln1_b, router_w, router_b, w_gate, b_gate, w_up, b_up, w_down, b_down, ln2_g, ln2_b):
    bsz, s, d = x.shape
    assert bsz == 1
    depth = w_ada.shape[0]
    xs = x.reshape(s, d)
    mod = _adaln(c.reshape(d, 1), w_ada, b_ada)

    n_a = 2 * A_WIDTH
    n_b = 3 * B_WIDTH
    w_main = jnp.concatenate(
        [w_in[:, :, n_a + n_b + B_HEADS:], w_in[:, :, :n_a + n_b]], axis=-1).astype(BF16)

    def split_pad(w):
        hi = w.astype(BF16)
        lo = (w - hi.astype(F32)).astype(BF16)
        pad = ((0, 0), (0, 0), (0, LANES - w.shape[-1]))
        return jnp.concatenate([jnp.pad(hi, pad), jnp.pad(lo, pad)], axis=-1)

    w_f = split_pad(w_in[:, :, n_a + n_b:n_a + n_b + B_HEADS])
    wa16 = w_proj_a.astype(BF16)
    wb16 = w_proj_b.astype(BF16)
    wo16 = w_o.astype(BF16)
    rw_pad = split_pad(router_w)
    rb_pad = jnp.pad(router_b, ((0, 0), (0, ROUTER_PAD - N_EXPERTS)), constant_values=float("-inf"))

    n_all = depth * N_EXPERTS
    wg_all = w_gate.reshape(n_all, d, D_EXPERT)
    wu_all = w_up.reshape(n_all, d, D_EXPERT)
    wd_all = w_down.reshape(n_all, D_EXPERT, d)
    bg_all = b_gate.reshape(n_all, 1, D_EXPERT)
    bu_all = b_up.reshape(n_all, 1, D_EXPERT)
    bd_all = b_down.reshape(n_all, 1, d)

    for l in range(depth):
        sh1, sc1, g1, sh2, sc2, g2 = [mod[l, :, k * d:(k + 1) * d] for k in range(6)]
        z, f_logit = _inproj(xs, sc1, sh1, w_main[l], w_f[l])
        q_sumsq, k_sumsq = _tile_norms(z)
        cum_hs = _forget_cumsum(f_logit.T, b_f[l].reshape(B_HEADS, 1))
        ya = _sgu(z, sgu_ln_g[l].reshape(1, A_WIDTH), spatial_w[l], spatial_b[l].T)
        yb = _attention(z, cum_hs, _prune_bounds(q_sumsq, k_sumsq, cum_hs))
        x1, h2, top_i, top_w = _post_mixer(
            ya, yb, z, xs, wa16[l], wb16[l], wo16[l], g1, ln1_g[l].reshape(1, d),
            ln1_b[l].reshape(1, d), sc2, sh2, rw_pad[l], rb_pad[l].reshape(1, ROUTER_PAD))
        pos, row_token, tile_expert, n_valid = _routing_tables(top_i[:, :TOP_K], s)
        y_sorted = _moe_experts(l, tile_expert, n_valid, row_token, h2, wg_all, bg_all, wu_all, bu_all,
                                wd_all, bd_all)
        xs = _combine(pos, top_w, x1, g2, ln2_g[l].reshape(1, d), ln2_b[l].reshape(1, d), y_sorted)
    return xs.reshape(bsz, s, d)
```

```python
import functools

import jax
import jax.numpy as jnp
from jax import lax
from jax.experimental import pallas as pl
from jax.experimental.pallas import tpu as pltpu

F32 = jnp.float32
BF16 = jnp.bfloat16
HIGHEST = lax.Precision.HIGHEST

D_MODEL = 2048
DEPTH = 4
CHUNK = 128
A_GROUPS = 8
A_WIDTH = 1024
B_HEADS = 8
B_HEAD_DIM = 128
B_WIDTH = 1024
N_EXPERTS = 32
TOP_K = 4
D_EXPERT = 512
SWIGLU_LIMIT = 7.0
SWIGLU_ALPHA = 1.702
DEEPNORM_ALPHA = (2.0 * DEPTH) ** 0.25
LN_EPS = 1e-5
MASK_VALUE = -1e30

LANES = 128
VMEM_LIMIT = 56 * 1024 * 1024

Z_COLS = 2 * D_MODEL + 2 * A_WIDTH + 3 * B_WIDTH
Z_GA, Z_GB, Z_U, Z_V, Z_Q, Z_K, Z_VB = 0, 2048, 4096, 5120, 6144, 7168, 8192

TN_IN = 1024
TN_SUB = 256
TM_IN = 1024
TM_SGU = 256
TQ = 512
PRUNE_MARGIN = 110.0
NORM_SLACK = 1.01
ATTN_ROW_GROUPS = 2
TM_POST = 256
TM_MOE = 256
TM_COMB = 128
CUM_CHUNK = 512
ROUTER_PAD = 128

U32 = jnp.uint32
N_SLABS = D_MODEL // (2 * LANES)
SLAB_PITCH = 9
GATHER_SLOTS = 3


def _bf16_bits(v):
    return lax.bitcast_convert_type(v.astype(BF16).astype(F32), U32)


def _store_slabs(ref, base, rows, value):
    for a in range(N_SLABS):
        hi = _bf16_bits(value[:, a * LANES:(a + 1) * LANES])
        lo = _bf16_bits(value[:, (a + N_SLABS) * LANES:(a + N_SLABS + 1) * LANES])
        ref[pl.ds(base + a, rows, stride=SLAB_PITCH), :] = hi | (lo >> 16)
    zeros = jnp.zeros((rows, LANES), U32)
    for a in range(N_SLABS, SLAB_PITCH):
        ref[pl.ds(base + a, rows, stride=SLAB_PITCH), :] = zeros


def _load_slab(ref, base, rows, a):
    w = ref[pl.ds(base + a, rows, stride=SLAB_PITCH), :]
    hi = lax.bitcast_convert_type(w & jnp.uint32(0xFFFF0000), F32)
    lo = lax.bitcast_convert_type(w << 16, F32)
    return hi, lo


def _split_bf16(v):
    hi = v.astype(BF16)
    lo = (v - hi.astype(F32)).astype(BF16)
    return hi, lo


def _cparams(sem, vmem=VMEM_LIMIT):
    return pltpu.CompilerParams(dimension_semantics=sem, vmem_limit_bytes=vmem)


def _adaln_body(c_ref, w_ref, b_ref, o_ref):
    c = c_ref[...]
    cond = c * jax.nn.sigmoid(c)
    o_ref[0] = jnp.sum(w_ref[0] * cond, axis=0, keepdims=True) + b_ref[0]


def _adaln(c_col, w_ada, b_ada):
    depth, d, n = w_ada.shape
    tn = 1024
    return pl.pallas_call(
        _adaln_body,
        grid=(depth, n // tn),
        in_specs=[
            pl.BlockSpec((d, 1), lambda l, j: (0, 0)),
            pl.BlockSpec((1, d, tn), lambda l, j: (l, 0, j)),
            pl.BlockSpec((1, 1, tn), lambda l, j: (l, 0, j)),
        ],
        out_specs=pl.BlockSpec((1, 1, tn), lambda l, j: (l, 0, j)),
        out_shape=jax.ShapeDtypeStruct((depth, 1, n), F32),
        compiler_params=_cparams(("arbitrary", "arbitrary")),
        name="adaln",
    )(c_col, w_ada, b_ada.reshape(depth, 1, n))


def _gelu_tanh(x):
    return 0.5 * x * (1.0 + jnp.tanh(0.7978845608028654 * (x + 0.044715 * (x * x * x))))


def _inproj_body(x_ref, sc_ref, sh_ref, w_ref, wf_ref, z_ref, f_ref, h_scr):
    j = pl.program_id(1)

    @pl.when(j == 0)
    def _():
        h = x_ref[...] * (1.0 + sc_ref[...]) + sh_ref[...]
        h_hi, h_lo = _split_bf16(h)
        h_scr[...] = h_hi
        r1 = jnp.dot(h_hi, wf_ref[...], preferred_element_type=F32)
        r2 = jnp.dot(h_lo, wf_ref[:, :LANES], preferred_element_type=F32)
        f_ref[...] = (r1[:, :LANES] + r1[:, LANES:] + r2)[:, :B_HEADS]

    n_sig = (2 * D_MODEL) // TN_IN
    n_gelu = (2 * A_WIDTH) // TN_IN
    n_sub = TN_IN // TN_SUB

    def project(epilogue):
        for c in range(n_sub):
            cols = slice(c * TN_SUB, (c + 1) * TN_SUB)
            acc = jnp.dot(h_scr[...], w_ref[:, cols], preferred_element_type=F32)
            z_ref[:, cols] = epilogue(acc).astype(BF16)

    @pl.when(j < n_sig)
    def _():
        project(jax.nn.sigmoid)

    @pl.when(jnp.logical_and(j >= n_sig, j < n_sig + n_gelu))
    def _():
        project(_gelu_tanh)

    @pl.when(j >= n_sig + n_gelu)
    def _():
        project(lambda a: a)


def _inproj(x, sc, sh, w_main, w_f):
    s, d = x.shape
    tm = min(TM_IN, s)
    return pl.pallas_call(
        _inproj_body,
        grid=(s // tm, Z_COLS // TN_IN),
        in_specs=[
            pl.BlockSpec((tm, d), lambda i, j: (i, 0)),
            pl.BlockSpec((1, d), lambda i, j: (0, 0)),
            pl.BlockSpec((1, d), lambda i, j: (0, 0)),
            pl.BlockSpec((d, TN_IN), lambda i, j: (0, j)),
            pl.BlockSpec((d, 2 * LANES), lambda i, j: (0, 0)),
        ],
        out_specs=[
            pl.BlockSpec((tm, TN_IN), lambda i, j: (i, j)),
            pl.BlockSpec((tm, B_HEADS), lambda i, j: (i, 0)),
        ],
        out_shape=[
            jax.ShapeDtypeStruct((s, Z_COLS), BF16),
            jax.ShapeDtypeStruct((s, B_HEADS), F32),
        ],
        scratch_shapes=[pltpu.VMEM((tm, d), BF16)],
        compiler_params=_cparams(("arbitrary", "arbitrary")),
        name="inproj",
    )(x, sc, sh, w_main, w_f)


def _cum_body(f_ref, bf_ref, o_ref):
    n_chunks = f_ref.shape[1] // CUM_CHUNK
    row = lax.broadcasted_iota(jnp.int32, (CUM_CHUNK, CUM_CHUNK), 0)
    col = lax.broadcasted_iota(jnp.int32, (CUM_CHUNK, CUM_CHUNK), 1)
    upper = jnp.where(row <= col, 1.0, 0.0).astype(F32)

    def body(i, carry):
        sl = pl.ds(pl.multiple_of(i * CUM_CHUNK, CUM_CHUNK), CUM_CHUNK)
        xf = f_ref[:, sl] + bf_ref[...]
        logf = jnp.minimum(xf, 0.0) - jnp.log1p(jnp.exp(-jnp.abs(xf)))
        cs = jnp.dot(logf, upper, precision=HIGHEST, preferred_element_type=F32) + carry
        o_ref[:, sl] = cs
        return cs[:, CUM_CHUNK - 1:CUM_CHUNK]

    lax.fori_loop(0, n_chunks, body, jnp.zeros((B_HEADS, 1), F32))


def _forget_cumsum(f_t, b_f_col):
    h, s = f_t.shape
    return pl.pallas_call(
        _cum_body,
        out_shape=jax.ShapeDtypeStruct((h, s), F32),
        compiler_params=pltpu.CompilerParams(vmem_limit_bytes=VMEM_LIMIT),
        name="forget_cumsum",
    )(f_t, b_f_col)


def _sgu_body(u_ref, v_ref, g_ref, w_ref, b_ref, o_ref):
    tm = u_ref.shape[0]
    row = lax.broadcasted_iota(jnp.int32, (CHUNK, CHUNK), 0)
    col = lax.broadcasted_iota(jnp.int32, (CHUNK, CHUNK), 1)
    causal = row >= col
    for g in range(A_GROUPS):
        cols = slice(g * CHUNK, (g + 1) * CHUNK)
        w = jnp.where(causal, w_ref[g], 0.0).astype(BF16)
        bias = b_ref[:, g:g + 1]
        gain = g_ref[:, cols]
        for ch in range(tm // CHUNK):
            rows = slice(ch * CHUNK, (ch + 1) * CHUNK)
            v = v_ref[rows, cols].astype(F32)
            mu = jnp.mean(v, axis=-1, keepdims=True)
            vc = v - mu
            var = jnp.mean(vc * vc, axis=-1, keepdims=True)
            vn = vc * lax.rsqrt(var + LN_EPS) * gain
            mixed = jnp.dot(w, vn.astype(BF16), preferred_element_type=F32) + bias
            o_ref[rows, cols] = (u_ref[rows, cols].astype(F32) * mixed).astype(BF16)


def _sgu(z, ln_g, w_s, b_s_t):
    s = z.shape[0]
    tm = min(TM_SGU, s)
    return pl.pallas_call(
        _sgu_body,
        grid=(s // tm,),
        in_specs=[
            pl.BlockSpec((tm, A_WIDTH), lambda i: (i, Z_U // A_WIDTH)),
            pl.BlockSpec((tm, A_WIDTH), lambda i: (i, Z_V // A_WIDTH)),
            pl.BlockSpec((1, A_WIDTH), lambda i: (0, 0)),
            pl.BlockSpec((A_GROUPS, CHUNK, CHUNK), lambda i: (0, 0, 0)),
            pl.BlockSpec((CHUNK, A_GROUPS), lambda i: (0, 0)),
        ],
        out_specs=pl.BlockSpec((tm, A_WIDTH), lambda i: (i, 0)),
        out_shape=jax.ShapeDtypeStruct((s, A_WIDTH), BF16),
        compiler_params=_cparams(("arbitrary",)),
        name="sgu",
    )(z, z, ln_g, w_s, b_s_t)


def _attn_body(lo_ref, q_ref, k_ref, v_ref, ck_ref, o_ref, m_scr, acc_scr, s_even, s_odd):
    h = pl.program_id(0)
    i = pl.program_id(1)
    tq = q_ref.shape[0]
    tk = tq
    scale = B_HEAD_DIM ** -0.5
    q = q_ref[...]
    c_ref = ck_ref[0, :, pl.ds(pl.multiple_of(i * tq, tq), LANES)][:, 0:1]
    ones = jnp.ones((tk, B_HEAD_DIM), BF16)
    lo = lo_ref[h, i]

    m_scr[...] = jnp.full((tq, 1), MASK_VALUE, F32)
    acc_scr[...] = jnp.zeros((tq, 2 * B_HEAD_DIM), F32)

    def key_slice(j):
        return pl.ds(pl.multiple_of(j * tk, tk), tk)

    def raw_scores(j):
        return lax.dot_general(q, k_ref[key_slice(j), :], (((1,), (1,)), ((), ())),
                               preferred_element_type=F32)

    def consume(s_ref, j, masked):
        ks = key_slice(j)
        v_aug = jnp.concatenate([v_ref[ks, :], ones], axis=1)
        bias = c_ref - ck_ref[0, :, ks]
        rows_per_group = tq // ATTN_ROW_GROUPS
        for g in range(ATTN_ROW_GROUPS):
            rows = slice(g * rows_per_group, (g + 1) * rows_per_group)
            s = s_ref[rows, :] * scale + bias
            if masked:
                r = lax.broadcasted_iota(jnp.int32, (rows_per_group, tk), 0) + g * rows_per_group
                c = lax.broadcasted_iota(jnp.int32, (rows_per_group, tk), 1)
                s = jnp.where(r >= c, s, MASK_VALUE)
            m_prev = m_scr[rows, :]
            m_new = jnp.maximum(m_prev, jnp.max(s, axis=1, keepdims=True))
            p = jnp.exp(s - m_new)
            alpha = jnp.exp(m_prev - m_new)
            acc_scr[rows, :] = alpha * acc_scr[rows, :] + jnp.dot(
                p.astype(BF16), v_aug, preferred_element_type=F32)
            m_scr[rows, :] = m_new

    s_even[...] = raw_scores(lo)

    def body(j, carry):
        even_visit = lax.rem(j - lo, 2) == 0

        @pl.when(even_visit)
        def _():
            s_odd[...] = raw_scores(j + 1)
            consume(s_even, j, False)

        @pl.when(jnp.logical_not(even_visit))
        def _():
            s_even[...] = raw_scores(j + 1)
            consume(s_odd, j, False)

        return carry

    lax.fori_loop(lo, i, body, 0)
    last_even = lax.rem(i - lo, 2) == 0

    @pl.when(last_even)
    def _():
        consume(s_even, i, True)

    @pl.when(jnp.logical_not(last_even))
    def _():
        consume(s_odd, i, True)

    acc = acc_scr[...]
    o_ref[...] = (acc[:, :B_HEAD_DIM] / acc[:, B_HEAD_DIM:]).astype(BF16)


def _attention(z, cum_hs, lo):
    s = z.shape[0]
    tq = min(TQ, s)
    grid_spec = pltpu.PrefetchScalarGridSpec(
        num_scalar_prefetch=1,
        grid=(B_HEADS, s // tq),
        in_specs=[
            pl.BlockSpec((tq, B_HEAD_DIM), lambda h, i, lo: (i, Z_Q // B_HEAD_DIM + h)),
            pl.BlockSpec((s, B_HEAD_DIM), lambda h, i, lo: (0, Z_K // B_HEAD_DIM + h)),
            pl.BlockSpec((s, B_HEAD_DIM), lambda h, i, lo: (0, Z_VB // B_HEAD_DIM + h)),
            pl.BlockSpec((1, 1, s), lambda h, i, lo: (h, 0, 0)),
        ],
        out_specs=pl.BlockSpec((tq, B_HEAD_DIM), lambda h, i, lo: (i, h)),
        scratch_shapes=[
            pltpu.VMEM((tq, 1), F32),
            pltpu.VMEM((tq, 2 * B_HEAD_DIM), F32),
            pltpu.VMEM((tq, tq), F32),
            pltpu.VMEM((tq, tq), F32),
        ],
    )
    return pl.pallas_call(
        _attn_body,
        grid_spec=grid_spec,
        out_shape=jax.ShapeDtypeStruct((s, B_WIDTH), BF16),
        compiler_params=_cparams(("arbitrary", "arbitrary")),
        name="fox_attention",
    )(lo, z, z, z, cum_hs.reshape(B_HEADS, 1, s))


def _tile_norm_body(q_ref, k_ref, qo_ref, ko_ref):
    col = lax.broadcasted_iota(jnp.int32, (B_WIDTH, LANES), 0)
    head = lax.broadcasted_iota(jnp.int32, (B_WIDTH, LANES), 1)
    sel = jnp.where(col // B_HEAD_DIM == head, 1.0, 0.0).astype(BF16)
    for src, dst in ((q_ref, qo_ref), (k_ref, ko_ref)):
        zc = src[...]
        sumsq = jnp.dot(zc * zc, sel, preferred_element_type=F32)
        dst[...] = jnp.broadcast_to(jnp.max(sumsq, axis=0, keepdims=True), dst.shape)


def _tile_norms(z):
    s = z.shape[0]
    tq = min(TQ, s)
    nblk = s // tq
    sub = 8
    out = jax.ShapeDtypeStruct((nblk * sub, LANES), F32)
    q_max, k_max = pl.pallas_call(
        _tile_norm_body,
        grid=(nblk,),
        in_specs=[
            pl.BlockSpec((tq, B_WIDTH), lambda i: (i, Z_Q // B_WIDTH)),
            pl.BlockSpec((tq, B_WIDTH), lambda i: (i, Z_K // B_WIDTH)),
        ],
        out_specs=[pl.BlockSpec((sub, LANES), lambda i: (i, 0)),
                   pl.BlockSpec((sub, LANES), lambda i: (i, 0))],
        out_shape=[out, out],
        compiler_params=_cparams(("arbitrary",)),
        name="tile_norms",
    )(z, z)
    pick = lambda t: t.reshape(nblk, sub, LANES)[:, 0, :B_HEADS]
    return pick(q_max), pick(k_max)


def _prune_bounds(q_sumsq, k_sumsq, cum_hs):
    s = cum_hs.shape[1]
    tq = min(TQ, s)
    nblk = s // tq
    scale = B_HEAD_DIM ** -0.5

    qn = jnp.sqrt(q_sumsq).T
    kn = jnp.sqrt(k_sumsq).T
    kpm = lax.cummax(kn, axis=1)
    c_first = cum_hs[:, ::tq]
    c_last = cum_hs[:, tq - 1::tq]
    dot_bound = scale * qn[:, :, None] * (kpm[:, None, :] + kn[:, :, None]) * NORM_SLACK
    bound = dot_bound + c_first[:, :, None] - c_last[:, None, :]
    tile = jnp.arange(nblk, dtype=jnp.int32)
    skip = jnp.logical_and(bound <= -PRUNE_MARGIN, tile[None, None, :] < tile[None, :, None])
    return jnp.sum(skip.astype(jnp.int32), axis=-1)


def _layer_norm(y, g, b):
    mu = jnp.mean(y, axis=-1, keepdims=True)
    yc = y - mu
    var = jnp.mean(yc * yc, axis=-1, keepdims=True)
    return yc * lax.rsqrt(var + LN_EPS) * g + b


def _post_body(ya_ref, yb_ref, ga_ref, gb_ref, x_ref, wa_ref, wb_ref, wo_ref, g1_ref, lng_ref,
               lnb_ref, sc_ref, sh_ref, rw_ref, rb_ref, x1_ref, h2_ref, ti_ref, tw_ref):
    tm = x_ref.shape[0]
    a = jnp.dot(ya_ref[...], wa_ref[...], preferred_element_type=F32)
    b = jnp.dot(yb_ref[...], wb_ref[...], preferred_element_type=F32)
    merged = ga_ref[...].astype(F32) * a + gb_ref[...].astype(F32) * b
    mix = jnp.dot(merged.astype(BF16), wo_ref[...], preferred_element_type=F32)
    x1 = _layer_norm(DEEPNORM_ALPHA * x_ref[...] + (1.0 + g1_ref[...]) * mix,
                     lng_ref[...], lnb_ref[...])
    x1_ref[...] = x1
    h2 = x1 * (1.0 + sc_ref[...]) + sh_ref[...]
    _store_slabs(h2_ref, 0, tm, h2)
    h_hi, h_lo = _split_bf16(h2)
    r1 = jnp.dot(h_hi, rw_ref[...], preferred_element_type=F32)
    r2 = jnp.dot(h_lo, rw_ref[:, :ROUTER_PAD], preferred_element_type=F32)
    logits = r1[:, :ROUTER_PAD] + r1[:, ROUTER_PAD:] + r2 + rb_ref[...]
    lane = lax.broadcasted_iota(jnp.int32, (tm, ROUTER_PAD), 1)
    lane_f = lane.astype(F32)
    neg_inf = float("-inf")
    work = logits
    vals, idxs = [], []
    for _ in range(TOP_K):
        m = jnp.max(work, axis=1, keepdims=True)
        idx = jnp.min(jnp.where(work == m, lane_f, float(ROUTER_PAD)), axis=1, keepdims=True)
        vals.append(m)
        idxs.append(idx)
        work = jnp.where(lane_f == idx, neg_inf, work)
    exps = [jnp.exp(v - vals[0]) for v in vals]
    denom = exps[0] + exps[1] + exps[2] + exps[3]
    ti = jnp.zeros((tm, ROUTER_PAD), F32)
    tw = jnp.zeros((tm, ROUTER_PAD), F32)
    for k in range(TOP_K):
        ti = jnp.where(lane == k, idxs[k], ti)
        tw = jnp.where(lane == k, exps[k] / denom, tw)
    ti_ref[...] = ti.astype(jnp.int32)
    tw_ref[...] = tw


def _post_mixer(ya, yb, z, x, wa, wb, wo, g1, ln_g, ln_b, sc2, sh2, rw_pad, rb_pad):
    s, d = x.shape
    tm = min(TM_POST, s)
    row = lambda i: (i, 0)
    const = lambda i: (0, 0)
    single = pl.Buffered(1)
    return pl.pallas_call(
        _post_body,
        grid=(s // tm,),
        in_specs=[
            pl.BlockSpec((tm, A_WIDTH), row),
            pl.BlockSpec((tm, B_WIDTH), row),
            pl.BlockSpec((tm, d), lambda i: (i, Z_GA // D_MODEL)),
            pl.BlockSpec((tm, d), lambda i: (i, Z_GB // D_MODEL)),
            pl.BlockSpec((tm, d), row),
            pl.BlockSpec((A_WIDTH, d), const, pipeline_mode=single),
            pl.BlockSpec((B_WIDTH, d), const, pipeline_mode=single),
            pl.BlockSpec((d, d), const, pipeline_mode=single),
            pl.BlockSpec((1, d), const),
            pl.BlockSpec((1, d), const),
            pl.BlockSpec((1, d), const),
            pl.BlockSpec((1, d), const),
            pl.BlockSpec((1, d), const),
            pl.BlockSpec((d, 2 * ROUTER_PAD), const),
            pl.BlockSpec((1, ROUTER_PAD), const),
        ],
        out_specs=[
            pl.BlockSpec((tm, d), row),
            pl.BlockSpec((tm * SLAB_PITCH, LANES), row),
            pl.BlockSpec((tm, ROUTER_PAD), row),
            pl.BlockSpec((tm, ROUTER_PAD), row),
        ],
        out_shape=[
            jax.ShapeDtypeStruct((s, d), F32),
            jax.ShapeDtypeStruct((s * SLAB_PITCH, LANES), U32),
            jax.ShapeDtypeStruct((s, ROUTER_PAD), jnp.int32),
            jax.ShapeDtypeStruct((s, ROUTER_PAD), F32),
        ],
        compiler_params=_cparams(("arbitrary",)),
        name="post_mixer",
    )(ya, yb, z, z, x, wa, wb, wo, g1, ln_g, ln_b, sc2, sh2, rw_pad, rb_pad)


def _slab_copy(src_hbm, dst_vmem, src_row, dst_row, sem):
    return pltpu.make_async_copy(src_hbm.at[pl.ds(src_row, N_SLABS)],
                                 dst_vmem.at[pl.ds(dst_row, N_SLABS)], sem)


def _gather_rows(src_hbm, dst_vmem, idx_ref, first, stop, dst_base, dst_row, sem, unrolled):
    if unrolled:
        for r in range(first, stop):
            _slab_copy(src_hbm, dst_vmem, idx_ref[0, 0, r], dst_base + dst_row(r) * SLAB_PITCH, sem).start()
    else:
        def issue(r, carry):
            _slab_copy(src_hbm, dst_vmem, idx_ref[0, 0, r], dst_base + dst_row(r) * SLAB_PITCH, sem).start()
            return carry

        lax.fori_loop(first, stop, issue, 0)


def _wait_rows(src_hbm, dst_vmem, n, dst_base, sem):
    def drain(r, carry):
        _slab_copy(src_hbm, dst_vmem, 0, dst_base + r * SLAB_PITCH, sem).wait()
        return carry

    lax.fori_loop(0, n, drain, 0, unroll=8)


def _moe_body(te_ref, nv_ref, rows0_ref, rows1_ref, rows2_ref, h_hbm, wg_ref, bg_ref, wu_ref, bu_ref,
              wd_ref, bd_ref, y_ref, xbuf, wg16, wu16, wd16, sem):
    i = pl.program_id(0)
    tm = xbuf.shape[0] // (GATHER_SLOTS * SLAB_PITCH)
    slot_rows = tm * SLAB_PITCH
    n_valid = nv_ref[0]
    same_row = lambda r: r

    def request(idx_ref, tile, unrolled):
        slot = lax.rem(tile, GATHER_SLOTS)

        @pl.when(tile < n_valid)
        def _():
            _gather_rows(h_hbm, xbuf, idx_ref, 0, tm, slot * slot_rows, same_row, sem.at[slot], unrolled)

    @pl.when(i == 0)
    def _():
        request(rows0_ref, i, False)
        request(rows1_ref, i + 1, False)

    request(rows2_ref, i + 2, True)

    @pl.when(i < n_valid)
    def _():
        new_expert = jnp.logical_or(i == 0, te_ref[i] != te_ref[jnp.maximum(i - 1, 0)])

        @pl.when(new_expert)
        def _():
            wg16[...] = wg_ref[0].astype(BF16)
            wu16[...] = wu_ref[0].astype(BF16)
            wd16[...] = wd_ref[0].astype(BF16)

        slot = lax.rem(i, GATHER_SLOTS)
        base = slot * slot_rows
        _wait_rows(h_hbm, xbuf, tm, base, sem.at[slot])
        halves = [_load_slab(xbuf, base, tm, a) for a in range(N_SLABS)]
        xb = jnp.concatenate([hi.astype(BF16) for hi, _ in halves] +
                             [lo.astype(BF16) for _, lo in halves], axis=1)
        g = jnp.minimum(jnp.dot(xb, wg16[...], preferred_element_type=F32) + bg_ref[0], SWIGLU_LIMIT)
        u = jnp.clip(jnp.dot(xb, wu16[...], preferred_element_type=F32) + bu_ref[0],
                     -SWIGLU_LIMIT, SWIGLU_LIMIT)
        act = (g * jax.nn.sigmoid(SWIGLU_ALPHA * g) * (u + 1.0)).astype(BF16)
        y = jnp.dot(act, wd16[...], preferred_element_type=F32) + bd_ref[0]
        _store_slabs(y_ref, 0, tm, y)

    @pl.when(i >= nv_ref[0])
    def _():
        y_ref[...] = jnp.zeros(y_ref.shape, U32)


def _moe_experts(layer, tile_expert, n_valid, row_token, h2, wg, bg, wu, bu, wd, bd):
    d = D_MODEL
    n_tiles = row_token.shape[0]
    tm = row_token.shape[2]
    wmap = lambda i, te, nv: (layer * N_EXPERTS + te[i], 0, 0)
    grid_spec = pltpu.PrefetchScalarGridSpec(
        num_scalar_prefetch=2,
        grid=(n_tiles,),
        in_specs=[
            pl.BlockSpec((1, 1, tm), lambda i, te, nv: (i, 0, 0), memory_space=pltpu.SMEM),
            pl.BlockSpec((1, 1, tm), lambda i, te, nv: (jnp.minimum(i + 1, n_tiles - 1), 0, 0),
                         memory_space=pltpu.SMEM),
            pl.BlockSpec((1, 1, tm), lambda i, te, nv: (jnp.minimum(i + 2, n_tiles - 1), 0, 0),
                         memory_space=pltpu.SMEM),
            pl.BlockSpec(memory_space=pl.ANY),
            pl.BlockSpec((1, d, D_EXPERT), wmap),
            pl.BlockSpec((1, 1, D_EXPERT), wmap),
            pl.BlockSpec((1, d, D_EXPERT), wmap),
            pl.BlockSpec((1, 1, D_EXPERT), wmap),
            pl.BlockSpec((1, D_EXPERT, d), wmap),
            pl.BlockSpec((1, 1, d), wmap),
        ],
        out_specs=pl.BlockSpec((tm * SLAB_PITCH, LANES), lambda i, te, nv: (i, 0)),
        scratch_shapes=[
            pltpu.VMEM((GATHER_SLOTS * tm * SLAB_PITCH, LANES), U32),
            pltpu.VMEM((d, D_EXPERT), BF16),
            pltpu.VMEM((d, D_EXPERT), BF16),
            pltpu.VMEM((D_EXPERT, d), BF16),
            pltpu.SemaphoreType.DMA((GATHER_SLOTS,)),
        ],
    )
    return pl.pallas_call(
        _moe_body,
        grid_spec=grid_spec,
        out_shape=jax.ShapeDtypeStruct((n_tiles * tm * SLAB_PITCH, LANES), U32),
        compiler_params=_cparams(("arbitrary",)),
        name="moe_experts",
    )(tile_expert, n_valid, row_token, row_token, row_token, h2, wg, bg, wu, bu, wd, bd)


def _combine_body(pos_ref, pos_next_ref, pos_next2_ref, tw_ref, x_ref, g2_ref, lng_ref, lnb_ref, y_hbm,
                  o_ref, ybuf, sem):
    i = pl.program_id(0)
    tm = x_ref.shape[0]
    n_rows = tm * TOP_K
    slot_rows = n_rows * SLAB_PITCH
    k_major = lambda r: (r % TOP_K) * tm + r // TOP_K

    def request(idx_ref, step, unrolled):
        slot = lax.rem(step, GATHER_SLOTS)

        @pl.when(step < pl.num_programs(0))
        def _():
            _gather_rows(y_hbm, ybuf, idx_ref, 0, n_rows, slot * slot_rows, k_major, sem.at[slot], unrolled)

    @pl.when(i == 0)
    def _():
        request(pos_ref, i, False)
        request(pos_next_ref, i + 1, False)

    request(pos_next2_ref, i + 2, True)

    slot = lax.rem(i, GATHER_SLOTS)
    base = slot * slot_rows
    _wait_rows(y_hbm, ybuf, n_rows, base, sem.at[slot])

    tw = tw_ref[...]
    first, second = [], []
    for a in range(N_SLABS):
        acc_hi = acc_lo = None
        for k in range(TOP_K):
            hi, lo = _load_slab(ybuf, base + k * tm * SLAB_PITCH, tm, a)
            wk = tw[:, k:k + 1]
            acc_hi = wk * hi if acc_hi is None else acc_hi + wk * hi
            acc_lo = wk * lo if acc_lo is None else acc_lo + wk * lo
        first.append(acc_hi)
        second.append(acc_lo)
    ffn = jnp.concatenate(first + second, axis=1)
    o_ref[...] = _layer_norm(DEEPNORM_ALPHA * x_ref[...] + (1.0 + g2_ref[...]) * ffn,
                             lng_ref[...], lnb_ref[...])


def _combine(pos, tw, x1, g2, ln_g, ln_b, y_sorted):
    s, d = x1.shape
    tm = min(TM_COMB, s)
    n_steps = s // tm
    row = lambda i: (i, 0)
    const = lambda i: (0, 0)
    pos3 = pos.reshape(n_steps, 1, tm * TOP_K)
    return pl.pallas_call(
        _combine_body,
        grid=(n_steps,),
        in_specs=[
            pl.BlockSpec((1, 1, tm * TOP_K), lambda i: (i, 0, 0), memory_space=pltpu.SMEM),
            pl.BlockSpec((1, 1, tm * TOP_K), lambda i: (jnp.minimum(i + 1, n_steps - 1), 0, 0),
                         memory_space=pltpu.SMEM),
            pl.BlockSpec((1, 1, tm * TOP_K), lambda i: (jnp.minimum(i + 2, n_steps - 1), 0, 0),
                         memory_space=pltpu.SMEM),
            pl.BlockSpec((tm, ROUTER_PAD), row),
            pl.BlockSpec((tm, d), row),
            pl.BlockSpec((1, d), const),
            pl.BlockSpec((1, d), const),
            pl.BlockSpec((1, d), const),
            pl.BlockSpec(memory_space=pl.ANY),
        ],
        out_specs=pl.BlockSpec((tm, d), row),
        out_shape=jax.ShapeDtypeStruct((s, d), F32),
        scratch_shapes=[pltpu.VMEM((GATHER_SLOTS * TOP_K * tm * SLAB_PITCH, LANES), U32),
                        pltpu.SemaphoreType.DMA((GATHER_SLOTS,))],
        compiler_params=_cparams(("arbitrary",)),
        name="moe_combine",
    )(pos3, pos3, pos3, tw, x1, g2, ln_g, ln_b, y_sorted)


def _routing_tables(top_i, s):
    tm = TM_MOE
    n_pairs = s * TOP_K
    n_tiles = n_pairs // tm + N_EXPERTS
    flat_e = top_i.reshape(n_pairs)
    onehot = (flat_e[:, None] == jnp.arange(N_EXPERTS, dtype=jnp.int32)[None, :]).astype(jnp.int32)
    csum = jnp.cumsum(onehot, axis=0)
    rank = jnp.sum(csum * onehot, axis=1) - 1
    counts = csum[-1]
    padded = ((counts + tm - 1) // tm) * tm
    ends = jnp.cumsum(padded)
    starts = ends - padded
    pos = jnp.sum(onehot * starts[None, :], axis=1) + rank
    row_token = jnp.zeros((n_tiles * tm,), jnp.int32).at[pos].set(
        (jnp.arange(n_pairs, dtype=jnp.int32) // TOP_K) * SLAB_PITCH,
        unique_indices=True, mode="promise_in_bounds")
    tile_start = jnp.arange(n_tiles, dtype=jnp.int32) * tm
    tile_expert = jnp.sum((tile_start[:, None] >= ends[None, :]).astype(jnp.int32), axis=1)
    tile_expert = jnp.minimum(tile_expert, N_EXPERTS - 1).astype(jnp.int32)
    n_valid = (ends[-1] // tm).astype(jnp.int32).reshape(1)
    pos_flat = (pos * SLAB_PITCH).astype(jnp.int32)
    return pos_flat, row_token.reshape(n_tiles, 1, tm), tile_expert, n_valid


def kernel(x, c, w_ada, b_ada, w_in, b_f, sgu_ln_g, spatial_w, spatial_b, w_proj_a, w_proj_b, w_o,
           ln1_g, ln1_b, router_w, router_b, w_gate, b_gate, w_up, b_up, w_down, b_down, ln2_g, ln2_b):
    bsz, s, d = x.shape
    assert bsz == 1
    depth = w_ada.shape[0]
    xs = x.reshape(s, d)
    mod = _adaln(c.reshape(d, 1), w_ada, b_ada)

    n_a = 2 * A_WIDTH
    n_b = 3 * B_WIDTH
    w_main = jnp.concatenate(
        [w_in[:, :, n_a + n_b + B_HEADS:], w_in[:, :, :n_a + n_b]], axis=-1).astype(BF16)

    def split_pad(w):
        hi = w.astype(BF16)
        lo = (w - hi.astype(F32)).astype(BF16)
        pad = ((0, 0), (0, 0), (0, LANES - w.shape[-1]))
        return jnp.concatenate([jnp.pad(hi, pad), jnp.pad(lo, pad)], axis=-1)

    w_f = split_pad(w_in[:, :, n_a + n_b:n_a + n_b + B_HEADS])
    wa16 = w_proj_a.astype(BF16)
    wb16 = w_proj_b.astype(BF16)
    wo16 = w_o.astype(BF16)
    rw_pad = split_pad(router_w)
    rb_pad = jnp.pad(router_b, ((0, 0), (0, ROUTER_PAD - N_EXPERTS)), constant_values=float("-inf"))

    n_all = depth * N_EXPERTS
    wg_all = w_gate.reshape(n_all, d, D_EXPERT)
    wu_all = w_up.reshape(n_all, d, D_EXPERT)
    wd_all = w_down.reshape(n_all, D_EXPERT, d)
    bg_all = b_gate.reshape(n_all, 1, D_EXPERT)
    bu_all = b_up.reshape(n_all, 1, D_EXPERT)
    bd_all = b_down.reshape(n_all, 1, d)

    for l in range(depth):
        sh1, sc1, g1, sh2, sc2, g2 = [mod[l, :, k * d:(k + 1) * d] for k in range(6)]
        z, f_logit = _inproj(xs, sc1, sh1, w_main[l], w_f[l])
        q_sumsq, k_sumsq = _tile_norms(z)
        cum_hs = _forget_cumsum(f_logit.T, b_f[l].reshape(B_HEADS, 1))
        ya = _sgu(z, sgu_ln_g[l].reshape(1, A_WIDTH), spatial_w[l], spatial_b[l].T)
        yb = _attention(z, cum_hs, _prune_bounds(q_sumsq, k_sumsq, cum_hs))
        x1, h2, top_i, top_w = _post_mixer(
            ya, yb, z, xs, wa16[l], wb16[l], wo16[l], g1, ln1_g[l].reshape(1, d),
            ln1_b[l].reshape(1, d), sc2, sh2, rw_pad[l], rb_pad[l].reshape(1, ROUTER_PAD))
        pos, row_token, tile_expert, n_valid = _routing_tables(top_i[:, :TOP_K], s)
        y_sorted = _moe_experts(l, tile_expert, n_valid, row_token, h2, wg_all, bg_all, wu_all, bu_all,
                                wd_all, bd_all)
        xs = _combine(pos, top_w, x1, g2, ln2_g[l].reshape(1, d), ln2_b[l].reshape(1, d), y_sorted)
    return xs.reshape(bsz, s, d)
```

```python
import functools

import jax
import jax.numpy as jnp
from jax import lax
from jax.experimental import pallas as pl
from jax.experimental.pallas import tpu as pltpu

F32 = jnp.float32
BF16 = jnp.bfloat16
HIGHEST = lax.Precision.HIGHEST

D_MODEL = 2048
DEPTH = 4
CHUNK = 128
A_GROUPS = 8
A_WIDTH = 1024
B_HEADS = 8
B_HEAD_DIM = 128
B_WIDTH = 1024
N_EXPERTS = 32
TOP_K = 4
D_EXPERT = 512
SWIGLU_LIMIT = 7.0
SWIGLU_ALPHA = 1.702
DEEPNORM_ALPHA = (2.0 * DEPTH) ** 0.25
LN_EPS = 1e-5
MASK_VALUE = -1e30

LANES = 128
VMEM_LIMIT = 56 * 1024 * 1024

Z_COLS = 2 * D_MODEL + 2 * A_WIDTH + 3 * B_WIDTH
Z_GA, Z_GB, Z_U, Z_V, Z_Q, Z_K, Z_VB = 0, 2048, 4096, 5120, 6144, 7168, 8192

TN_IN = 1024
TN_SUB = 256
TM_IN = 1024
TM_SGU = 256
TQ = 512
PRUNE_MARGIN = 110.0
NORM_SLACK = 1.01
ATTN_ROW_GROUPS = 2
TM_POST = 256
TM_MOE = 256
TM_COMB = 256
CUM_CHUNK = 512
ROUTER_PAD = 128

U32 = jnp.uint32
N_SLABS = D_MODEL // (2 * LANES)
SLAB_PITCH = 9
GATHER_SLOTS = 3


def _bf16_bits(v):
    return lax.bitcast_convert_type(v.astype(BF16).astype(F32), U32)


def _store_slabs(ref, base, rows, value):
    for a in range(N_SLABS):
        hi = _bf16_bits(value[:, a * LANES:(a + 1) * LANES])
        lo = _bf16_bits(value[:, (a + N_SLABS) * LANES:(a + N_SLABS + 1) * LANES])
        ref[pl.ds(base + a, rows, stride=SLAB_PITCH), :] = hi | (lo >> 16)
    zeros = jnp.zeros((rows, LANES), U32)
    for a in range(N_SLABS, SLAB_PITCH):
        ref[pl.ds(base + a, rows, stride=SLAB_PITCH), :] = zeros


def _load_slab(ref, base, rows, a):
    w = ref[pl.ds(base + a, rows, stride=SLAB_PITCH), :]
    hi = lax.bitcast_convert_type(w & jnp.uint32(0xFFFF0000), F32)
    lo = lax.bitcast_convert_type(w << 16, F32)
    return hi, lo


def _split_bf16(v):
    hi = v.astype(BF16)
    lo = (v - hi.astype(F32)).astype(BF16)
    return hi, lo


def _cparams(sem, vmem=VMEM_LIMIT):
    return pltpu.CompilerParams(dimension_semantics=sem, vmem_limit_bytes=vmem)


def _adaln_body(c_ref, w_ref, b_ref, o_ref):
    c = c_ref[...]
    cond = c * jax.nn.sigmoid(c)
    o_ref[0] = jnp.sum(w_ref[0] * cond, axis=0, keepdims=True) + b_ref[0]


def _adaln(c_col, w_ada, b_ada):
    depth, d, n = w_ada.shape
    tn = 1024
    return pl.pallas_call(
        _adaln_body,
        grid=(depth, n // tn),
        in_specs=[
            pl.BlockSpec((d, 1), lambda l, j: (0, 0)),
            pl.BlockSpec((1, d, tn), lambda l, j: (l, 0, j)),
            pl.BlockSpec((1, 1, tn), lambda l, j: (l, 0, j)),
        ],
        out_specs=pl.BlockSpec((1, 1, tn), lambda l, j: (l, 0, j)),
        out_shape=jax.ShapeDtypeStruct((depth, 1, n), F32),
        compiler_params=_cparams(("arbitrary", "arbitrary")),
        name="adaln",
    )(c_col, w_ada, b_ada.reshape(depth, 1, n))


def _gelu_tanh(x):
    return 0.5 * x * (1.0 + jnp.tanh(0.7978845608028654 * (x + 0.044715 * (x * x * x))))


def _inproj_body(x_ref, sc_ref, sh_ref, wgate_ref, wrest_ref, wf_ref, z_ref, f_ref, h_scr):
    j = pl.program_id(1)

    @pl.when(j == 0)
    def _():
        h = x_ref[...] * (1.0 + sc_ref[...]) + sh_ref[...]
        h_hi, h_lo = _split_bf16(h)
        h_scr[...] = h_hi
        r1 = jnp.dot(h_hi, wf_ref[...], preferred_element_type=F32)
        r2 = jnp.dot(h_lo, wf_ref[:, :LANES], preferred_element_type=F32)
        f_ref[...] = (r1[:, :LANES] + r1[:, LANES:] + r2)[:, :B_HEADS]

    n_sig = (2 * D_MODEL) // TN_IN
    n_gelu = (2 * A_WIDTH) // TN_IN
    n_sub = TN_IN // TN_SUB

    def project(epilogue, w_ref):
        for c in range(n_sub):
            cols = slice(c * TN_SUB, (c + 1) * TN_SUB)
            acc = jnp.dot(h_scr[...], w_ref[:, cols], preferred_element_type=F32)
            z_ref[:, cols] = epilogue(acc).astype(BF16)

    @pl.when(j < n_sig)
    def _():
        project(jax.nn.sigmoid, wgate_ref)

    @pl.when(jnp.logical_and(j >= n_sig, j < n_sig + n_gelu))
    def _():
        project(_gelu_tanh, wrest_ref)

    @pl.when(j >= n_sig + n_gelu)
    def _():
        project(lambda a: a, wrest_ref)


def _inproj(x, sc, sh, w_gates, w_rest, w_f):
    s, d = x.shape
    tm = min(TM_IN, s)
    n_sig = (2 * D_MODEL) // TN_IN
    n_rest = w_rest.shape[1] // TN_IN
    return pl.pallas_call(
        _inproj_body,
        grid=(s // tm, Z_COLS // TN_IN),
        in_specs=[
            pl.BlockSpec((tm, d), lambda i, j: (i, 0)),
            pl.BlockSpec((1, d), lambda i, j: (0, 0)),
            pl.BlockSpec((1, d), lambda i, j: (0, 0)),
            pl.BlockSpec((d, TN_IN), lambda i, j: (0, jnp.minimum(j, n_sig - 1))),
            pl.BlockSpec((d, TN_IN), lambda i, j: (0, jnp.clip(j - n_sig, 0, n_rest - 1))),
            pl.BlockSpec((d, 2 * LANES), lambda i, j: (0, 0)),
        ],
        out_specs=[
            pl.BlockSpec((tm, TN_IN), lambda i, j: (i, j)),
            pl.BlockSpec((tm, B_HEADS), lambda i, j: (i, 0)),
        ],
        out_shape=[
            jax.ShapeDtypeStruct((s, Z_COLS), BF16),
            jax.ShapeDtypeStruct((s, B_HEADS), F32),
        ],
        scratch_shapes=[pltpu.VMEM((tm, d), BF16)],
        compiler_params=_cparams(("arbitrary", "arbitrary")),
        name="inproj",
    )(x, sc, sh, w_gates, w_rest, w_f)


def _cum_body(f_ref, bf_ref, o_ref):
    n_chunks = f_ref.shape[1] // CUM_CHUNK
    row = lax.broadcasted_iota(jnp.int32, (CUM_CHUNK, CUM_CHUNK), 0)
    col = lax.broadcasted_iota(jnp.int32, (CUM_CHUNK, CUM_CHUNK), 1)
    upper = jnp.where(row <= col, 1.0, 0.0).astype(F32)

    def body(i, carry):
        sl = pl.ds(pl.multiple_of(i * CUM_CHUNK, CUM_CHUNK), CUM_CHUNK)
        xf = f_ref[:, sl] + bf_ref[...]
        logf = jnp.minimum(xf, 0.0) - jnp.log1p(jnp.exp(-jnp.abs(xf)))
        cs = jnp.dot(logf, upper, precision=HIGHEST, preferred_element_type=F32) + carry
        o_ref[:, sl] = cs
        return cs[:, CUM_CHUNK - 1:CUM_CHUNK]

    lax.fori_loop(0, n_chunks, body, jnp.zeros((B_HEADS, 1), F32))


def _forget_cumsum(f_t, b_f_col):
    h, s = f_t.shape
    return pl.pallas_call(
        _cum_body,
        out_shape=jax.ShapeDtypeStruct((h, s), F32),
        compiler_params=pltpu.CompilerParams(vmem_limit_bytes=VMEM_LIMIT),
        name="forget_cumsum",
    )(f_t, b_f_col)


def _sgu_body(u_ref, v_ref, g_ref, w_ref, b_ref, o_ref):
    tm = u_ref.shape[0]
    row = lax.broadcasted_iota(jnp.int32, (CHUNK, CHUNK), 0)
    col = lax.broadcasted_iota(jnp.int32, (CHUNK, CHUNK), 1)
    causal = row >= col
    for g in range(A_GROUPS):
        cols = slice(g * CHUNK, (g + 1) * CHUNK)
        w = jnp.where(causal, w_ref[g], 0.0).astype(BF16)
        bias = b_ref[:, g:g + 1]
        gain = g_ref[:, cols]
        for ch in range(tm // CHUNK):
            rows = slice(ch * CHUNK, (ch + 1) * CHUNK)
            v = v_ref[rows, cols].astype(F32)
            mu = jnp.mean(v, axis=-1, keepdims=True)
            vc = v - mu
            var = jnp.mean(vc * vc, axis=-1, keepdims=True)
            vn = vc * lax.rsqrt(var + LN_EPS) * gain
            mixed = jnp.dot(w, vn.astype(BF16), preferred_element_type=F32) + bias
            o_ref[rows, cols] = (u_ref[rows, cols].astype(F32) * mixed).astype(BF16)


def _sgu(z, ln_g, w_s, b_s_t):
    s = z.shape[0]
    tm = min(TM_SGU, s)
    return pl.pallas_call(
        _sgu_body,
        grid=(s // tm,),
        in_specs=[
            pl.BlockSpec((tm, A_WIDTH), lambda i: (i, Z_U // A_WIDTH)),
            pl.BlockSpec((tm, A_WIDTH), lambda i: (i, Z_V // A_WIDTH)),
            pl.BlockSpec((1, A_WIDTH), lambda i: (0, 0)),
            pl.BlockSpec((A_GROUPS, CHUNK, CHUNK), lambda i: (0, 0, 0)),
            pl.BlockSpec((CHUNK, A_GROUPS), lambda i: (0, 0)),
        ],
        out_specs=pl.BlockSpec((tm, A_WIDTH), lambda i: (i, 0)),
        out_shape=jax.ShapeDtypeStruct((s, A_WIDTH), BF16),
        compiler_params=_cparams(("arbitrary",)),
        name="sgu",
    )(z, z, ln_g, w_s, b_s_t)


def _attn_body(lo_ref, q_ref, k_ref, v_ref, ck_ref, o_ref, m_scr, acc_scr, s_even, s_odd):
    h = pl.program_id(0)
    i = pl.program_id(1)
    tq = q_ref.shape[0]
    tk = tq
    scale = B_HEAD_DIM ** -0.5
    q = q_ref[...]
    c_ref = ck_ref[0, :, pl.ds(pl.multiple_of(i * tq, tq), LANES)][:, 0:1]
    ones = jnp.ones((tk, B_HEAD_DIM), BF16)
    lo = lo_ref[h, i]

    m_scr[...] = jnp.full((tq, 1), MASK_VALUE, F32)
    acc_scr[...] = jnp.zeros((tq, 2 * B_HEAD_DIM), F32)

    def key_slice(j):
        return pl.ds(pl.multiple_of(j * tk, tk), tk)

    def raw_scores(j):
        return lax.dot_general(q, k_ref[key_slice(j), :], (((1,), (1,)), ((), ())),
                               preferred_element_type=F32)

    def consume(s_ref, j, masked):
        ks = key_slice(j)
        v_aug = jnp.concatenate([v_ref[ks, :], ones], axis=1)
        bias = c_ref - ck_ref[0, :, ks]
        rows_per_group = tq // ATTN_ROW_GROUPS
        for g in range(ATTN_ROW_GROUPS):
            rows = slice(g * rows_per_group, (g + 1) * rows_per_group)
            n_keys = (g + 1) * rows_per_group if masked else tk
            s = s_ref[rows, :n_keys] * scale + bias[:, :n_keys]
            if masked:
                r = lax.broadcasted_iota(jnp.int32, (rows_per_group, n_keys), 0) + g * rows_per_group
                c = lax.broadcasted_iota(jnp.int32, (rows_per_group, n_keys), 1)
                s = jnp.where(r >= c, s, MASK_VALUE)
            m_prev = m_scr[rows, :]
            m_new = jnp.maximum(m_prev, jnp.max(s, axis=1, keepdims=True))
            p = jnp.exp(s - m_new)
            alpha = jnp.exp(m_prev - m_new)
            acc_scr[rows, :] = alpha * acc_scr[rows, :] + jnp.dot(
                p.astype(BF16), v_aug[:n_keys, :], preferred_element_type=F32)
            m_scr[rows, :] = m_new

    s_even[...] = raw_scores(lo)

    def body(j, carry):
        even_visit = lax.rem(j - lo, 2) == 0

        @pl.when(even_visit)
        def _():
            s_odd[...] = raw_scores(j + 1)
            consume(s_even, j, False)

        @pl.when(jnp.logical_not(even_visit))
        def _():
            s_even[...] = raw_scores(j + 1)
            consume(s_odd, j, False)

        return carry

    lax.fori_loop(lo, i, body, 0)
    last_even = lax.rem(i - lo, 2) == 0

    @pl.when(last_even)
    def _():
        consume(s_even, i, True)

    @pl.when(jnp.logical_not(last_even))
    def _():
        consume(s_odd, i, True)

    acc = acc_scr[...]
    o_ref[...] = (acc[:, :B_HEAD_DIM] / acc[:, B_HEAD_DIM:]).astype(BF16)


def _attention(z, cum_hs, lo):
    s = z.shape[0]
    tq = min(TQ, s)
    grid_spec = pltpu.PrefetchScalarGridSpec(
        num_scalar_prefetch=1,
        grid=(B_HEADS, s // tq),
        in_specs=[
            pl.BlockSpec((tq, B_HEAD_DIM), lambda h, i, lo: (i, Z_Q // B_HEAD_DIM + h)),
            pl.BlockSpec((s, B_HEAD_DIM), lambda h, i, lo: (0, Z_K // B_HEAD_DIM + h)),
            pl.BlockSpec((s, B_HEAD_DIM), lambda h, i, lo: (0, Z_VB // B_HEAD_DIM + h)),
            pl.BlockSpec((1, 1, s), lambda h, i, lo: (h, 0, 0)),
        ],
        out_specs=pl.BlockSpec((tq, B_HEAD_DIM), lambda h, i, lo: (i, h)),
        scratch_shapes=[
            pltpu.VMEM((tq, 1), F32),
            pltpu.VMEM((tq, 2 * B_HEAD_DIM), F32),
            pltpu.VMEM((tq, tq), F32),
            pltpu.VMEM((tq, tq), F32),
        ],
    )
    return pl.pallas_call(
        _attn_body,
        grid_spec=grid_spec,
        out_shape=jax.ShapeDtypeStruct((s, B_WIDTH), BF16),
        compiler_params=_cparams(("arbitrary", "arbitrary")),
        name="fox_attention",
    )(lo, z, z, z, cum_hs.reshape(B_HEADS, 1, s))


def _tile_norm_body(q_ref, k_ref, qo_ref, ko_ref):
    col = lax.broadcasted_iota(jnp.int32, (B_WIDTH, LANES), 0)
    head = lax.broadcasted_iota(jnp.int32, (B_WIDTH, LANES), 1)
    sel = jnp.where(col // B_HEAD_DIM == head, 1.0, 0.0).astype(BF16)
    for src, dst in ((q_ref, qo_ref), (k_ref, ko_ref)):
        zc = src[...]
        sumsq = jnp.dot(zc * zc, sel, preferred_element_type=F32)
        dst[...] = jnp.broadcast_to(jnp.max(sumsq, axis=0, keepdims=True), dst.shape)


def _tile_norms(z):
    s = z.shape[0]
    tq = min(TQ, s)
    nblk = s // tq
    sub = 8
    out = jax.ShapeDtypeStruct((nblk * sub, LANES), F32)
    q_max, k_max = pl.pallas_call(
        _tile_norm_body,
        grid=(nblk,),
        in_specs=[
            pl.BlockSpec((tq, B_WIDTH), lambda i: (i, Z_Q // B_WIDTH)),
            pl.BlockSpec((tq, B_WIDTH), lambda i: (i, Z_K // B_WIDTH)),
        ],
        out_specs=[pl.BlockSpec((sub, LANES), lambda i: (i, 0)),
                   pl.BlockSpec((sub, LANES), lambda i: (i, 0))],
        out_shape=[out, out],
        compiler_params=_cparams(("arbitrary",)),
        name="tile_norms",
    )(z, z)
    pick = lambda t: t.reshape(nblk, sub, LANES)[:, 0, :B_HEADS]
    return pick(q_max), pick(k_max)


def _prune_bounds(q_sumsq, k_sumsq, cum_hs):
    s = cum_hs.shape[1]
    tq = min(TQ, s)
    nblk = s // tq
    scale = B_HEAD_DIM ** -0.5

    qn = jnp.sqrt(q_sumsq).T
    kn = jnp.sqrt(k_sumsq).T
    kpm = lax.cummax(kn, axis=1)
    c_first = cum_hs[:, ::tq]
    c_last = cum_hs[:, tq - 1::tq]
    dot_bound = scale * qn[:, :, None] * (kpm[:, None, :] + kn[:, :, None]) * NORM_SLACK
    bound = dot_bound + c_first[:, :, None] - c_last[:, None, :]
    tile = jnp.arange(nblk, dtype=jnp.int32)
    skip = jnp.logical_and(bound <= -PRUNE_MARGIN, tile[None, None, :] < tile[None, :, None])
    return jnp.sum(skip.astype(jnp.int32), axis=-1)


def _layer_norm(y, g, b):
    mu = jnp.mean(y, axis=-1, keepdims=True)
    yc = y - mu
    var = jnp.mean(yc * yc, axis=-1, keepdims=True)
    return yc * lax.rsqrt(var + LN_EPS) * g + b


def _post_body(ya_ref, yb_ref, ga_ref, gb_ref, x_ref, wa_ref, wb_ref, wo_ref, g1_ref, lng_ref,
               lnb_ref, sc_ref, sh_ref, rw_ref, rb_ref, x1_ref, h2_ref, ti_ref, tw_ref):
    tm = x_ref.shape[0]
    a = jnp.dot(ya_ref[...], wa_ref[...], preferred_element_type=F32)
    b = jnp.dot(yb_ref[...], wb_ref[...], preferred_element_type=F32)
    merged = ga_ref[...].astype(F32) * a + gb_ref[...].astype(F32) * b
    mix = jnp.dot(merged.astype(BF16), wo_ref[...], preferred_element_type=F32)
    x1 = _layer_norm(DEEPNORM_ALPHA * x_ref[...] + (1.0 + g1_ref[...]) * mix,
                     lng_ref[...], lnb_ref[...])
    x1_ref[...] = x1
    h2 = x1 * (1.0 + sc_ref[...]) + sh_ref[...]
    _store_slabs(h2_ref, 0, tm, h2)
    h_hi, h_lo = _split_bf16(h2)
    r1 = jnp.dot(h_hi, rw_ref[...], preferred_element_type=F32)
    r2 = jnp.dot(h_lo, rw_ref[:, :ROUTER_PAD], preferred_element_type=F32)
    logits = r1[:, :ROUTER_PAD] + r1[:, ROUTER_PAD:] + r2 + rb_ref[...]
    lane = lax.broadcasted_iota(jnp.int32, (tm, ROUTER_PAD), 1)
    lane_f = lane.astype(F32)
    neg_inf = float("-inf")
    work = logits
    vals, idxs = [], []
    for _ in range(TOP_K):
        m = jnp.max(work, axis=1, keepdims=True)
        idx = jnp.min(jnp.where(work == m, lane_f, float(ROUTER_PAD)), axis=1, keepdims=True)
        vals.append(m)
        idxs.append(idx)
        work = jnp.where(lane_f == idx, neg_inf, work)
    exps = [jnp.exp(v - vals[0]) for v in vals]
    denom = exps[0] + exps[1] + exps[2] + exps[3]
    ti = jnp.zeros((tm, ROUTER_PAD), F32)
    tw = jnp.zeros((tm, ROUTER_PAD), F32)
    for k in range(TOP_K):
        ti = jnp.where(lane == k, idxs[k], ti)
        tw = jnp.where(lane == k, exps[k] / denom, tw)
    ti_ref[...] = ti.astype(jnp.int32)
    tw_ref[...] = tw


def _post_mixer(ya, yb, z, x, wa, wb, wo, g1, ln_g, ln_b, sc2, sh2, rw_pad, rb_pad):
    s, d = x.shape
    tm = min(TM_POST, s)
    row = lambda i: (i, 0)
    const = lambda i: (0, 0)
    single = pl.Buffered(1)
    return pl.pallas_call(
        _post_body,
        grid=(s // tm,),
        in_specs=[
            pl.BlockSpec((tm, A_WIDTH), row),
            pl.BlockSpec((tm, B_WIDTH), row),
            pl.BlockSpec((tm, d), lambda i: (i, Z_GA // D_MODEL)),
            pl.BlockSpec((tm, d), lambda i: (i, Z_GB // D_MODEL)),
            pl.BlockSpec((tm, d), row),
            pl.BlockSpec((A_WIDTH, d), const, pipeline_mode=single),
            pl.BlockSpec((B_WIDTH, d), const, pipeline_mode=single),
            pl.BlockSpec((d, d), const, pipeline_mode=single),
            pl.BlockSpec((1, d), const),
            pl.BlockSpec((1, d), const),
            pl.BlockSpec((1, d), const),
            pl.BlockSpec((1, d), const),
            pl.BlockSpec((1, d), const),
            pl.BlockSpec((d, 2 * ROUTER_PAD), const),
            pl.BlockSpec((1, ROUTER_PAD), const),
        ],
        out_specs=[
            pl.BlockSpec((tm, d), row),
            pl.BlockSpec((tm * SLAB_PITCH, LANES), row),
            pl.BlockSpec((tm, ROUTER_PAD), row),
            pl.BlockSpec((tm, ROUTER_PAD), row),
        ],
        out_shape=[
            jax.ShapeDtypeStruct((s, d), F32),
            jax.ShapeDtypeStruct((s * SLAB_PITCH, LANES), U32),
            jax.ShapeDtypeStruct((s, ROUTER_PAD), jnp.int32),
            jax.ShapeDtypeStruct((s, ROUTER_PAD), F32),
        ],
        compiler_params=_cparams(("arbitrary",)),
        name="post_mixer",
    )(ya, yb, z, z, x, wa, wb, wo, g1, ln_g, ln_b, sc2, sh2, rw_pad, rb_pad)


def _slab_copy(src_hbm, dst_vmem, src_row, dst_row, sem):
    return pltpu.make_async_copy(src_hbm.at[pl.ds(src_row, N_SLABS)],
                                 dst_vmem.at[pl.ds(dst_row, N_SLABS)], sem)


def _gather_rows(src_hbm, dst_vmem, idx_ref, first, stop, dst_base, dst_row, sem, unrolled):
    if unrolled:
        for r in range(first, stop):
            _slab_copy(src_hbm, dst_vmem, idx_ref[0, 0, r], dst_base + dst_row(r) * SLAB_PITCH, sem).start()
    else:
        def issue(r, carry):
            _slab_copy(src_hbm, dst_vmem, idx_ref[0, 0, r], dst_base + dst_row(r) * SLAB_PITCH, sem).start()
            return carry

        lax.fori_loop(first, stop, issue, 0)


def _wait_rows(src_hbm, dst_vmem, n, dst_base, sem):
    def drain(r, carry):
        _slab_copy(src_hbm, dst_vmem, 0, dst_base + r * SLAB_PITCH, sem).wait()
        return carry

    lax.fori_loop(0, n, drain, 0, unroll=8)


def _moe_body(te_ref, nv_ref, rows0_ref, rows1_ref, rows2_ref, h_hbm, wg_ref, bg_ref, wu_ref, bu_ref,
              wd_ref, bd_ref, y_ref, xbuf, wg16, wu16, wd16, sem):
    i = pl.program_id(0)
    tm = xbuf.shape[0] // (GATHER_SLOTS * SLAB_PITCH)
    slot_rows = tm * SLAB_PITCH
    n_valid = nv_ref[0]
    same_row = lambda r: r

    def request(idx_ref, tile, unrolled):
        slot = lax.rem(tile, GATHER_SLOTS)

        @pl.when(tile < n_valid)
        def _():
            _gather_rows(h_hbm, xbuf, idx_ref, 0, tm, slot * slot_rows, same_row, sem.at[slot], unrolled)

    @pl.when(i == 0)
    def _():
        request(rows0_ref, i, False)
        request(rows1_ref, i + 1, False)

    request(rows2_ref, i + 2, True)

    @pl.when(i < n_valid)
    def _():
        new_expert = jnp.logical_or(i == 0, te_ref[i] != te_ref[jnp.maximum(i - 1, 0)])

        @pl.when(new_expert)
        def _():
            wg16[...] = wg_ref[0].astype(BF16)
            wu16[...] = wu_ref[0].astype(BF16)
            wd16[...] = wd_ref[0].astype(BF16)

        slot = lax.rem(i, GATHER_SLOTS)
        base = slot * slot_rows
        _wait_rows(h_hbm, xbuf, tm, base, sem.at[slot])
        halves = [_load_slab(xbuf, base, tm, a) for a in range(N_SLABS)]
        xb = jnp.concatenate([hi.astype(BF16) for hi, _ in halves] +
                             [lo.astype(BF16) for _, lo in halves], axis=1)
        g = jnp.minimum(jnp.dot(xb, wg16[...], preferred_element_type=F32) + bg_ref[0], SWIGLU_LIMIT)
        u = jnp.clip(jnp.dot(xb, wu16[...], preferred_element_type=F32) + bu_ref[0],
                     -SWIGLU_LIMIT, SWIGLU_LIMIT)
        act = (g * jax.nn.sigmoid(SWIGLU_ALPHA * g) * (u + 1.0)).astype(BF16)
        y = jnp.dot(act, wd16[...], preferred_element_type=F32) + bd_ref[0]
        _store_slabs(y_ref, 0, tm, y)

    @pl.when(i >= nv_ref[0])
    def _():
        y_ref[...] = jnp.zeros(y_ref.shape, U32)


def _moe_experts(layer, tile_expert, n_valid, row_token, h2, wg, bg, wu, bu, wd, bd):
    d = D_MODEL
    n_tiles = row_token.shape[0]
    tm = row_token.shape[2]
    wmap = lambda i, te, nv: (layer * N_EXPERTS + te[i], 0, 0)
    grid_spec = pltpu.PrefetchScalarGridSpec(
        num_scalar_prefetch=2,
        grid=(n_tiles,),
        in_specs=[
            pl.BlockSpec((1, 1, tm), lambda i, te, nv: (i, 0, 0), memory_space=pltpu.SMEM),
            pl.BlockSpec((1, 1, tm), lambda i, te, nv: (jnp.minimum(i + 1, n_tiles - 1), 0, 0),
                         memory_space=pltpu.SMEM),
            pl.BlockSpec((1, 1, tm), lambda i, te, nv: (jnp.minimum(i + 2, n_tiles - 1), 0, 0),
                         memory_space=pltpu.SMEM),
            pl.BlockSpec(memory_space=pl.ANY),
            pl.BlockSpec((1, d, D_EXPERT), wmap),
            pl.BlockSpec((1, 1, D_EXPERT), wmap),
            pl.BlockSpec((1, d, D_EXPERT), wmap),
            pl.BlockSpec((1, 1, D_EXPERT), wmap),
            pl.BlockSpec((1, D_EXPERT, d), wmap),
            pl.BlockSpec((1, 1, d), wmap),
        ],
        out_specs=pl.BlockSpec((tm * SLAB_PITCH, LANES), lambda i, te, nv: (i, 0)),
        scratch_shapes=[
            pltpu.VMEM((GATHER_SLOTS * tm * SLAB_PITCH, LANES), U32),
            pltpu.VMEM((d, D_EXPERT), BF16),
            pltpu.VMEM((d, D_EXPERT), BF16),
            pltpu.VMEM((D_EXPERT, d), BF16),
            pltpu.SemaphoreType.DMA((GATHER_SLOTS,)),
        ],
    )
    return pl.pallas_call(
        _moe_body,
        grid_spec=grid_spec,
        out_shape=jax.ShapeDtypeStruct((n_tiles * tm * SLAB_PITCH, LANES), U32),
        compiler_params=_cparams(("arbitrary",)),
        name="moe_experts",
    )(tile_expert, n_valid, row_token, row_token, row_token, h2, wg, bg, wu, bu, wd, bd)


def _combine_body(pos_ref, pos_next_ref, pos_next2_ref, tw_ref, x_ref, g2_ref, lng_ref, lnb_ref, y_hbm,
                  o_ref, ybuf, sem):
    i = pl.program_id(0)
    tm = x_ref.shape[0]
    n_rows = tm * TOP_K
    slot_rows = n_rows * SLAB_PITCH
    k_major = lambda r: (r % TOP_K) * tm + r // TOP_K

    def request(idx_ref, step, unrolled):
        slot = lax.rem(step, GATHER_SLOTS)

        @pl.when(step < pl.num_programs(0))
        def _():
            _gather_rows(y_hbm, ybuf, idx_ref, 0, n_rows, slot * slot_rows, k_major, sem.at[slot], unrolled)

    @pl.when(i == 0)
    def _():
        request(pos_ref, i, False)
        request(pos_next_ref, i + 1, False)

    request(pos_next2_ref, i + 2, True)

    slot = lax.rem(i, GATHER_SLOTS)
    base = slot * slot_rows
    _wait_rows(y_hbm, ybuf, n_rows, base, sem.at[slot])

    tw = tw_ref[...]
    first, second = [], []
    for a in range(N_SLABS):
        acc_hi = acc_lo = None
        for k in range(TOP_K):
            hi, lo = _load_slab(ybuf, base + k * tm * SLAB_PITCH, tm, a)
            wk = tw[:, k:k + 1]
            acc_hi = wk * hi if acc_hi is None else acc_hi + wk * hi
            acc_lo = wk * lo if acc_lo is None else acc_lo + wk * lo
        first.append(acc_hi)
        second.append(acc_lo)
    ffn = jnp.concatenate(first + second, axis=1)
    o_ref[...] = _layer_norm(DEEPNORM_ALPHA * x_ref[...] + (1.0 + g2_ref[...]) * ffn,
                             lng_ref[...], lnb_ref[...])


def _combine(pos, tw, x1, g2, ln_g, ln_b, y_sorted):
    s, d = x1.shape
    tm = min(TM_COMB, s)
    n_steps = s // tm
    row = lambda i: (i, 0)
    const = lambda i: (0, 0)
    pos3 = pos.reshape(n_steps, 1, tm * TOP_K)
    return pl.pallas_call(
        _combine_body,
        grid=(n_steps,),
        in_specs=[
            pl.BlockSpec((1, 1, tm * TOP_K), lambda i: (i, 0, 0), memory_space=pltpu.SMEM),
            pl.BlockSpec((1, 1, tm * TOP_K), lambda i: (jnp.minimum(i + 1, n_steps - 1), 0, 0),
                         memory_space=pltpu.SMEM),
            pl.BlockSpec((1, 1, tm * TOP_K), lambda i: (jnp.minimum(i + 2, n_steps - 1), 0, 0),
                         memory_space=pltpu.SMEM),
            pl.BlockSpec((tm, ROUTER_PAD), row),
            pl.BlockSpec((tm, d), row),
            pl.BlockSpec((1, d), const),
            pl.BlockSpec((1, d), const),
            pl.BlockSpec((1, d), const),
            pl.BlockSpec(memory_space=pl.ANY),
        ],
        out_specs=pl.BlockSpec((tm, d), row),
        out_shape=jax.ShapeDtypeStruct((s, d), F32),
        scratch_shapes=[pltpu.VMEM((GATHER_SLOTS * TOP_K * tm * SLAB_PITCH, LANES), U32),
                        pltpu.SemaphoreType.DMA((GATHER_SLOTS,))],
        compiler_params=_cparams(("arbitrary",)),
        name="moe_combine",
    )(pos3, pos3, pos3, tw, x1, g2, ln_g, ln_b, y_sorted)


def _routing_tables(top_i, s):
    tm = TM_MOE
    n_pairs = s * TOP_K
    n_tiles = n_pairs // tm + N_EXPERTS
    flat_e = top_i.reshape(n_pairs)
    onehot = (flat_e[:, None] == jnp.arange(N_EXPERTS, dtype=jnp.int32)[None, :]).astype(jnp.int32)
    csum = jnp.cumsum(onehot, axis=0)
    rank = jnp.sum(csum * onehot, axis=1) - 1
    counts = csum[-1]
    padded = ((counts + tm - 1) // tm) * tm
    ends = jnp.cumsum(padded)
    starts = ends - padded
    pos = jnp.sum(onehot * starts[None, :], axis=1) + rank
    row_token = jnp.zeros((n_tiles * tm,), jnp.int32).at[pos].set(
        (jnp.arange(n_pairs, dtype=jnp.int32) // TOP_K) * SLAB_PITCH,
        unique_indices=True, mode="promise_in_bounds")
    tile_start = jnp.arange(n_tiles, dtype=jnp.int32) * tm
    tile_expert = jnp.sum((tile_start[:, None] >= ends[None, :]).astype(jnp.int32), axis=1)
    tile_expert = jnp.minimum(tile_expert, N_EXPERTS - 1).astype(jnp.int32)
    n_valid = (ends[-1] // tm).astype(jnp.int32).reshape(1)
    pos_flat = (pos * SLAB_PITCH).astype(jnp.int32)
    return pos_flat, row_token.reshape(n_tiles, 1, tm), tile_expert, n_valid


def kernel(x, c, w_ada, b_ada, w_in, b_f, sgu_ln_g, spatial_w, spatial_b, w_proj_a, w_proj_b, w_o,
           ln1_g, ln1_b, router_w, router_b, w_gate, b_gate, w_up, b_up, w_down, b_down, ln2_g, ln2_b):
    bsz, s, d = x.shape
    assert bsz == 1
    depth = w_ada.shape[0]
    xs = x.reshape(s, d)
    mod = _adaln(c.reshape(d, 1), w_ada, b_ada)

    n_a = 2 * A_WIDTH
    n_b = 3 * B_WIDTH
    w_gates = w_in[:, :, n_a + n_b + B_HEADS:].astype(BF16)
    w_rest = w_in[:, :, :n_a + n_b].astype(BF16)

    def split_pad(w):
        hi = w.astype(BF16)
        lo = (w - hi.astype(F32)).astype(BF16)
        pad = ((0, 0), (0, 0), (0, LANES - w.shape[-1]))
        return jnp.concatenate([jnp.pad(hi, pad), jnp.pad(lo, pad)], axis=-1)

    w_f = split_pad(w_in[:, :, n_a + n_b:n_a + n_b + B_HEADS])
    wa16 = w_proj_a.astype(BF16)
    wb16 = w_proj_b.astype(BF16)
    wo16 = w_o.astype(BF16)
    rw_pad = split_pad(router_w)
    rb_pad = jnp.pad(router_b, ((0, 0), (0, ROUTER_PAD - N_EXPERTS)), constant_values=float("-inf"))

    n_all = depth * N_EXPERTS
    wg_all = w_gate.reshape(n_all, d, D_EXPERT)
    wu_all = w_up.reshape(n_all, d, D_EXPERT)
    wd_all = w_down.reshape(n_all, D_EXPERT, d)
    bg_all = b_gate.reshape(n_all, 1, D_EXPERT)
    bu_all = b_up.reshape(n_all, 1, D_EXPERT)
    bd_all = b_down.reshape(n_all, 1, d)

    for l in range(depth):
        sh1, sc1, g1, sh2, sc2, g2 = [mod[l, :, k * d:(k + 1) * d] for k in range(6)]
        z, f_logit = _inproj(xs, sc1, sh1, w_gates[l], w_rest[l], w_f[l])
        q_sumsq, k_sumsq = _tile_norms(z)
        cum_hs = _forget_cumsum(f_logit.T, b_f[l].reshape(B_HEADS, 1))
        ya = _sgu(z, sgu_ln_g[l].reshape(1, A_WIDTH), spatial_w[l], spatial_b[l].T)
        yb = _attention(z, cum_hs, _prune_bounds(q_sumsq, k_sumsq, cum_hs))
        x1, h2, top_i, top_w = _post_mixer(
            ya, yb, z, xs, wa16[l], wb16[l], wo16[l], g1, ln1_g[l].reshape(1, d),
            ln1_b[l].reshape(1, d), sc2, sh2, rw_pad[l], rb_pad[l].reshape(1, ROUTER_PAD))
        pos, row_token, tile_expert, n_valid = _routing_tables(top_i[:, :TOP_K], s)
        y_sorted = _moe_experts(l, tile_expert, n_valid, row_token, h2, wg_all, bg_all, wu_all, bu_all,
                                wd_all, bd_all)
        xs = _combine(pos, top_w, x1, g2, ln2_g[l].reshape(1, d), ln2_b[l].reshape(1, d), y_sorted)
    return xs.reshape(bsz, s, d)
```

```python
import functools

import jax
import jax.numpy as jnp
from jax import lax
from jax.experimental import pallas as pl
from jax.experimental.pallas import tpu as pltpu

F32 = jnp.float32
BF16 = jnp.bfloat16
HIGHEST = lax.Precision.HIGHEST

D_MODEL = 2048
DEPTH = 4
CHUNK = 128
A_GROUPS = 8
A_WIDTH = 1024
B_HEADS = 8
B_HEAD_DIM = 128
B_WIDTH = 1024
N_EXPERTS = 32
TOP_K = 4
D_EXPERT = 512
SWIGLU_LIMIT = 7.0
SWIGLU_ALPHA = 1.702
DEEPNORM_ALPHA = (2.0 * DEPTH) ** 0.25
LN_EPS = 1e-5
MASK_VALUE = -1e30

LANES = 128
VMEM_LIMIT = 56 * 1024 * 1024

Z_COLS = 2 * D_MODEL + 2 * A_WIDTH + 3 * B_WIDTH
Z_GA, Z_GB, Z_U, Z_V, Z_Q, Z_K, Z_VB = 0, 2048, 4096, 5120, 6144, 7168, 8192

TN_IN = 1024
TN_SUB = 256
TM_IN = 1024
TM_SGU = 256
TQ = 512
PRUNE_MARGIN = 110.0
NORM_SLACK = 1.01
ATTN_ROW_GROUPS = 2
TM_POST = 256
TM_MOE = 256
TM_COMB = 128
CUM_CHUNK = 512
ROUTER_PAD = 128

U32 = jnp.uint32
N_SLABS = D_MODEL // (2 * LANES)
SLAB_PITCH = 9
GATHER_SLOTS = 3


def _bf16_bits(v):
    return lax.bitcast_convert_type(v.astype(BF16).astype(F32), U32)


def _store_slabs(ref, base, rows, value):
    for a in range(N_SLABS):
        hi = _bf16_bits(value[:, a * LANES:(a + 1) * LANES])
        lo = _bf16_bits(value[:, (a + N_SLABS) * LANES:(a + N_SLABS + 1) * LANES])
        ref[pl.ds(base + a, rows, stride=SLAB_PITCH), :] = hi | (lo >> 16)
    zeros = jnp.zeros((rows, LANES), U32)
    for a in range(N_SLABS, SLAB_PITCH):
        ref[pl.ds(base + a, rows, stride=SLAB_PITCH), :] = zeros


def _load_slab(ref, base, rows, a):
    w = ref[pl.ds(base + a, rows, stride=SLAB_PITCH), :]
    hi = lax.bitcast_convert_type(w & jnp.uint32(0xFFFF0000), F32)
    lo = lax.bitcast_convert_type(w << 16, F32)
    return hi, lo


def _split_bf16(v):
    hi = v.astype(BF16)
    lo = (v - hi.astype(F32)).astype(BF16)
    return hi, lo


def _cparams(sem, vmem=VMEM_LIMIT):
    return pltpu.CompilerParams(dimension_semantics=sem, vmem_limit_bytes=vmem)


def _adaln_body(c_ref, w_ref, b_ref, o_ref):
    c = c_ref[...]
    cond = c * jax.nn.sigmoid(c)
    o_ref[0] = jnp.sum(w_ref[0] * cond, axis=0, keepdims=True) + b_ref[0]


def _adaln(c_col, w_ada, b_ada):
    depth, d, n = w_ada.shape
    tn = 1024
    return pl.pallas_call(
        _adaln_body,
        grid=(depth, n // tn),
        in_specs=[
            pl.BlockSpec((d, 1), lambda l, j: (0, 0)),
            pl.BlockSpec((1, d, tn), lambda l, j: (l, 0, j)),
            pl.BlockSpec((1, 1, tn), lambda l, j: (l, 0, j)),
        ],
        out_specs=pl.BlockSpec((1, 1, tn), lambda l, j: (l, 0, j)),
        out_shape=jax.ShapeDtypeStruct((depth, 1, n), F32),
        compiler_params=_cparams(("arbitrary", "arbitrary")),
        name="adaln",
    )(c_col, w_ada, b_ada.reshape(depth, 1, n))


def _gelu_tanh(x):
    return 0.5 * x * (1.0 + jnp.tanh(0.7978845608028654 * (x + 0.044715 * (x * x * x))))


def _inproj_body(x_ref, sc_ref, sh_ref, w_ref, wf_ref, z_ref, f_ref, h_scr):
    j = pl.program_id(1)

    @pl.when(j == 0)
    def _():
        h = x_ref[...] * (1.0 + sc_ref[...]) + sh_ref[...]
        h_hi, h_lo = _split_bf16(h)
        h_scr[...] = h_hi
        r1 = jnp.dot(h_hi, wf_ref[...], preferred_element_type=F32)
        r2 = jnp.dot(h_lo, wf_ref[:, :LANES], preferred_element_type=F32)
        f_ref[...] = (r1[:, :LANES] + r1[:, LANES:] + r2)[:, :B_HEADS]

    n_sig = (2 * D_MODEL) // TN_IN
    n_gelu = (2 * A_WIDTH) // TN_IN
    n_sub = TN_IN // TN_SUB

    def project(epilogue):
        for c in range(n_sub):
            cols = slice(c * TN_SUB, (c + 1) * TN_SUB)
            acc = jnp.dot(h_scr[...], w_ref[:, cols], preferred_element_type=F32)
            z_ref[:, cols] = epilogue(acc).astype(BF16)

    @pl.when(j < n_sig)
    def _():
        project(jax.nn.sigmoid)

    @pl.when(jnp.logical_and(j >= n_sig, j < n_sig + n_gelu))
    def _():
        project(_gelu_tanh)

    @pl.when(j >= n_sig + n_gelu)
    def _():
        project(lambda a: a)


def _inproj(x, sc, sh, w_main, w_f):
    s, d = x.shape
    tm = min(TM_IN, s)
    return pl.pallas_call(
        _inproj_body,
        grid=(s // tm, Z_COLS // TN_IN),
        in_specs=[
            pl.BlockSpec((tm, d), lambda i, j: (i, 0)),
            pl.BlockSpec((1, d), lambda i, j: (0, 0)),
            pl.BlockSpec((1, d), lambda i, j: (0, 0)),
            pl.BlockSpec((d, TN_IN), lambda i, j: (0, j)),
            pl.BlockSpec((d, 2 * LANES), lambda i, j: (0, 0)),
        ],
        out_specs=[
            pl.BlockSpec((tm, TN_IN), lambda i, j: (i, j)),
            pl.BlockSpec((tm, B_HEADS), lambda i, j: (i, 0)),
        ],
        out_shape=[
            jax.ShapeDtypeStruct((s, Z_COLS), BF16),
            jax.ShapeDtypeStruct((s, B_HEADS), F32),
        ],
        scratch_shapes=[pltpu.VMEM((tm, d), BF16)],
        compiler_params=_cparams(("arbitrary", "arbitrary")),
        name="inproj",
    )(x, sc, sh, w_main, w_f)


def _cum_body(f_ref, bf_ref, o_ref):
    n_chunks = f_ref.shape[1] // CUM_CHUNK
    row = lax.broadcasted_iota(jnp.int32, (CUM_CHUNK, CUM_CHUNK), 0)
    col = lax.broadcasted_iota(jnp.int32, (CUM_CHUNK, CUM_CHUNK), 1)
    upper = jnp.where(row <= col, 1.0, 0.0).astype(F32)

    def body(i, carry):
        sl = pl.ds(pl.multiple_of(i * CUM_CHUNK, CUM_CHUNK), CUM_CHUNK)
        xf = f_ref[:, sl] + bf_ref[...]
        logf = jnp.minimum(xf, 0.0) - jnp.log1p(jnp.exp(-jnp.abs(xf)))
        cs = jnp.dot(logf, upper, precision=HIGHEST, preferred_element_type=F32) + carry
        o_ref[:, sl] = cs
        return cs[:, CUM_CHUNK - 1:CUM_CHUNK]

    lax.fori_loop(0, n_chunks, body, jnp.zeros((B_HEADS, 1), F32))


def _forget_cumsum(f_t, b_f_col):
    h, s = f_t.shape
    return pl.pallas_call(
        _cum_body,
        out_shape=jax.ShapeDtypeStruct((h, s), F32),
        compiler_params=pltpu.CompilerParams(vmem_limit_bytes=VMEM_LIMIT),
        name="forget_cumsum",
    )(f_t, b_f_col)


def _sgu_body(u_ref, v_ref, g_ref, w_ref, b_ref, o_ref):
    tm = u_ref.shape[0]
    row = lax.broadcasted_iota(jnp.int32, (CHUNK, CHUNK), 0)
    col = lax.broadcasted_iota(jnp.int32, (CHUNK, CHUNK), 1)
    causal = row >= col
    for g in range(A_GROUPS):
        cols = slice(g * CHUNK, (g + 1) * CHUNK)
        w = jnp.where(causal, w_ref[g], 0.0).astype(BF16)
        bias = b_ref[:, g:g + 1]
        gain = g_ref[:, cols]
        for ch in range(tm // CHUNK):
            rows = slice(ch * CHUNK, (ch + 1) * CHUNK)
            v = v_ref[rows, cols].astype(F32)
            mu = jnp.mean(v, axis=-1, keepdims=True)
            vc = v - mu
            var = jnp.mean(vc * vc, axis=-1, keepdims=True)
            vn = vc * lax.rsqrt(var + LN_EPS) * gain
            mixed = jnp.dot(w, vn.astype(BF16), preferred_element_type=F32) + bias
            o_ref[rows, cols] = (u_ref[rows, cols].astype(F32) * mixed).astype(BF16)


def _sgu(z, ln_g, w_s, b_s_t):
    s = z.shape[0]
    tm = min(TM_SGU, s)
    return pl.pallas_call(
        _sgu_body,
        grid=(s // tm,),
        in_specs=[
            pl.BlockSpec((tm, A_WIDTH), lambda i: (i, Z_U // A_WIDTH)),
            pl.BlockSpec((tm, A_WIDTH), lambda i: (i, Z_V // A_WIDTH)),
            pl.BlockSpec((1, A_WIDTH), lambda i: (0, 0)),
            pl.BlockSpec((A_GROUPS, CHUNK, CHUNK), lambda i: (0, 0, 0)),
            pl.BlockSpec((CHUNK, A_GROUPS), lambda i: (0, 0)),
        ],
        out_specs=pl.BlockSpec((tm, A_WIDTH), lambda i: (i, 0)),
        out_shape=jax.ShapeDtypeStruct((s, A_WIDTH), BF16),
        compiler_params=_cparams(("arbitrary",)),
        name="sgu",
    )(z, z, ln_g, w_s, b_s_t)


def _attn_body(lo_ref, q_ref, k_ref, v_ref, ck_ref, o_ref, m_scr, acc_scr, s_even, s_odd):
    h = pl.program_id(0)
    i = pl.program_id(1)
    tq = q_ref.shape[0]
    tk = tq
    scale = B_HEAD_DIM ** -0.5
    q = q_ref[...]
    c_ref = ck_ref[0, :, pl.ds(pl.multiple_of(i * tq, tq), LANES)][:, 0:1]
    ones = jnp.ones((tk, B_HEAD_DIM), BF16)
    lo = lo_ref[h, i]

    m_scr[...] = jnp.full((tq, 1), MASK_VALUE, F32)
    acc_scr[...] = jnp.zeros((tq, 2 * B_HEAD_DIM), F32)

    def key_slice(j):
        return pl.ds(pl.multiple_of(j * tk, tk), tk)

    def raw_scores(j):
        return lax.dot_general(q, k_ref[key_slice(j), :], (((1,), (1,)), ((), ())),
                               preferred_element_type=F32)

    def consume(s_ref, j, masked):
        ks = key_slice(j)
        v_aug = jnp.concatenate([v_ref[ks, :], ones], axis=1)
        bias = c_ref - ck_ref[0, :, ks]
        rows_per_group = tq // ATTN_ROW_GROUPS
        for g in range(ATTN_ROW_GROUPS):
            rows = slice(g * rows_per_group, (g + 1) * rows_per_group)
            s = s_ref[rows, :] * scale + bias
            if masked:
                r = lax.broadcasted_iota(jnp.int32, (rows_per_group, tk), 0) + g * rows_per_group
                c = lax.broadcasted_iota(jnp.int32, (rows_per_group, tk), 1)
                s = jnp.where(r >= c, s, MASK_VALUE)
            m_prev = m_scr[rows, :]
            m_new = jnp.maximum(m_prev, jnp.max(s, axis=1, keepdims=True))
            p = jnp.exp(s - m_new)
            alpha = jnp.exp(m_prev - m_new)
            acc_scr[rows, :] = alpha * acc_scr[rows, :] + jnp.dot(
                p.astype(BF16), v_aug, preferred_element_type=F32)
            m_scr[rows, :] = m_new

    s_even[...] = raw_scores(lo)

    def body(j, carry):
        even_visit = lax.rem(j - lo, 2) == 0

        @pl.when(even_visit)
        def _():
            s_odd[...] = raw_scores(j + 1)
            consume(s_even, j, False)

        @pl.when(jnp.logical_not(even_visit))
        def _():
            s_even[...] = raw_scores(j + 1)
            consume(s_odd, j, False)

        return carry

    lax.fori_loop(lo, i, body, 0)
    last_even = lax.rem(i - lo, 2) == 0

    @pl.when(last_even)
    def _():
        consume(s_even, i, True)

    @pl.when(jnp.logical_not(last_even))
    def _():
        consume(s_odd, i, True)

    acc = acc_scr[...]
    o_ref[...] = (acc[:, :B_HEAD_DIM] / acc[:, B_HEAD_DIM:]).astype(BF16)


def _attention(z, cum_hs, lo):
    s = z.shape[0]
    tq = min(TQ, s)
    grid_spec = pltpu.PrefetchScalarGridSpec(
        num_scalar_prefetch=1,
        grid=(B_HEADS, s // tq),
        in_specs=[
            pl.BlockSpec((tq, B_HEAD_DIM), lambda h, i, lo: (i, Z_Q // B_HEAD_DIM + h)),
            pl.BlockSpec((s, B_HEAD_DIM), lambda h, i, lo: (0, Z_K // B_HEAD_DIM + h)),
            pl.BlockSpec((s, B_HEAD_DIM), lambda h, i, lo: (0, Z_VB // B_HEAD_DIM + h)),
            pl.BlockSpec((1, 1, s), lambda h, i, lo: (h, 0, 0)),
        ],
        out_specs=pl.BlockSpec((tq, B_HEAD_DIM), lambda h, i, lo: (i, h)),
        scratch_shapes=[
            pltpu.VMEM((tq, 1), F32),
            pltpu.VMEM((tq, 2 * B_HEAD_DIM), F32),
            pltpu.VMEM((tq, tq), F32),
            pltpu.VMEM((tq, tq), F32),
        ],
    )
    return pl.pallas_call(
        _attn_body,
        grid_spec=grid_spec,
        out_shape=jax.ShapeDtypeStruct((s, B_WIDTH), BF16),
        compiler_params=_cparams(("arbitrary", "arbitrary")),
        name="fox_attention",
    )(lo, z, z, z, cum_hs.reshape(B_HEADS, 1, s))


def _tile_norm_body(q_ref, k_ref, qo_ref, ko_ref):
    col = lax.broadcasted_iota(jnp.int32, (B_WIDTH, LANES), 0)
    head = lax.broadcasted_iota(jnp.int32, (B_WIDTH, LANES), 1)
    sel = jnp.where(col // B_HEAD_DIM == head, 1.0, 0.0).astype(BF16)
    for src, dst in ((q_ref, qo_ref), (k_ref, ko_ref)):
        zc = src[...]
        sumsq = jnp.dot(zc * zc, sel, preferred_element_type=F32)
        dst[...] = jnp.broadcast_to(jnp.max(sumsq, axis=0, keepdims=True), dst.shape)


def _tile_norms(z):
    s = z.shape[0]
    tq = min(TQ, s)
    nblk = s // tq
    sub = 8
    out = jax.ShapeDtypeStruct((nblk * sub, LANES), F32)
    q_max, k_max = pl.pallas_call(
        _tile_norm_body,
        grid=(nblk,),
        in_specs=[
            pl.BlockSpec((tq, B_WIDTH), lambda i: (i, Z_Q // B_WIDTH)),
            pl.BlockSpec((tq, B_WIDTH), lambda i: (i, Z_K // B_WIDTH)),
        ],
        out_specs=[pl.BlockSpec((sub, LANES), lambda i: (i, 0)),
                   pl.BlockSpec((sub, LANES), lambda i: (i, 0))],
        out_shape=[out, out],
        compiler_params=_cparams(("arbitrary",)),
        name="tile_norms",
    )(z, z)
    pick = lambda t: t.reshape(nblk, sub, LANES)[:, 0, :B_HEADS]
    return pick(q_max), pick(k_max)


def _prune_bounds(q_sumsq, k_sumsq, cum_hs):
    s = cum_hs.shape[1]
    tq = min(TQ, s)
    nblk = s // tq
    scale = B_HEAD_DIM ** -0.5

    qn = jnp.sqrt(q_sumsq).T
    kn = jnp.sqrt(k_sumsq).T
    kpm = lax.cummax(kn, axis=1)
    c_first = cum_hs[:, ::tq]
    c_last = cum_hs[:, tq - 1::tq]
    dot_bound = scale * qn[:, :, None] * (kpm[:, None, :] + kn[:, :, None]) * NORM_SLACK
    bound = dot_bound + c_first[:, :, None] - c_last[:, None, :]
    tile = jnp.arange(nblk, dtype=jnp.int32)
    skip = jnp.logical_and(bound <= -PRUNE_MARGIN, tile[None, None, :] < tile[None, :, None])
    return jnp.sum(skip.astype(jnp.int32), axis=-1)


def _layer_norm(y, g, b):
    mu = jnp.mean(y, axis=-1, keepdims=True)
    yc = y - mu
    var = jnp.mean(yc * yc, axis=-1, keepdims=True)
    return yc * lax.rsqrt(var + LN_EPS) * g + b


def _post_body(ya_ref, yb_ref, ga_ref, gb_ref, x_ref, wa_ref, wb_ref, wo_ref, g1_ref, lng_ref,
               lnb_ref, sc_ref, sh_ref, rw_ref, rb_ref, x1_ref, h2_ref, ti_ref, tw_ref):
    tm = x_ref.shape[0]
    a = jnp.dot(ya_ref[...], wa_ref[...], preferred_element_type=F32)
    b = jnp.dot(yb_ref[...], wb_ref[...], preferred_element_type=F32)
    merged = ga_ref[...].astype(F32) * a + gb_ref[...].astype(F32) * b
    mix = jnp.dot(merged.astype(BF16), wo_ref[...], preferred_element_type=F32)
    x1 = _layer_norm(DEEPNORM_ALPHA * x_ref[...] + (1.0 + g1_ref[...]) * mix,
                     lng_ref[...], lnb_ref[...])
    x1_ref[...] = x1
    h2 = x1 * (1.0 + sc_ref[...]) + sh_ref[...]
    _store_slabs(h2_ref, 0, tm, h2)
    h_hi, h_lo = _split_bf16(h2)
    r1 = jnp.dot(h_hi, rw_ref[...], preferred_element_type=F32)
    r2 = jnp.dot(h_lo, rw_ref[:, :ROUTER_PAD], preferred_element_type=F32)
    logits = r1[:, :ROUTER_PAD] + r1[:, ROUTER_PAD:] + r2 + rb_ref[...]
    lane = lax.broadcasted_iota(jnp.int32, (tm, ROUTER_PAD), 1)
    lane_f = lane.astype(F32)
    neg_inf = float("-inf")
    work = logits
    vals, idxs = [], []
    for _ in range(TOP_K):
        m = jnp.max(work, axis=1, keepdims=True)
        idx = jnp.min(jnp.where(work == m, lane_f, float(ROUTER_PAD)), axis=1, keepdims=True)
        vals.append(m)
        idxs.append(idx)
        work = jnp.where(lane_f == idx, neg_inf, work)
    exps = [jnp.exp(v - vals[0]) for v in vals]
    denom = exps[0] + exps[1] + exps[2] + exps[3]
    ti = jnp.zeros((tm, ROUTER_PAD), F32)
    tw = jnp.zeros((tm, ROUTER_PAD), F32)
    for k in range(TOP_K):
        ti = jnp.where(lane == k, idxs[k], ti)
        tw = jnp.where(lane == k, exps[k] / denom, tw)
    ti_ref[...] = ti.astype(jnp.int32)
    tw_ref[...] = tw


def _post_mixer(ya, yb, z, x, wa, wb, wo, g1, ln_g, ln_b, sc2, sh2, rw_pad, rb_pad):
    s, d = x.shape
    tm = min(TM_POST, s)
    row = lambda i: (i, 0)
    const = lambda i: (0, 0)
    single = pl.Buffered(1)
    return pl.pallas_call(
        _post_body,
        grid=(s // tm,),
        in_specs=[
            pl.BlockSpec((tm, A_WIDTH), row),
            pl.BlockSpec((tm, B_WIDTH), row),
            pl.BlockSpec((tm, d), lambda i: (i, Z_GA // D_MODEL)),
            pl.BlockSpec((tm, d), lambda i: (i, Z_GB // D_MODEL)),
            pl.BlockSpec((tm, d), row),
            pl.BlockSpec((A_WIDTH, d), const, pipeline_mode=single),
            pl.BlockSpec((B_WIDTH, d), const, pipeline_mode=single),
            pl.BlockSpec((d, d), const, pipeline_mode=single),
            pl.BlockSpec((1, d), const),
            pl.BlockSpec((1, d), const),
            pl.BlockSpec((1, d), const),
            pl.BlockSpec((1, d), const),
            pl.BlockSpec((1, d), const),
            pl.BlockSpec((d, 2 * ROUTER_PAD), const),
            pl.BlockSpec((1, ROUTER_PAD), const),
        ],
        out_specs=[
            pl.BlockSpec((tm, d), row),
            pl.BlockSpec((tm * SLAB_PITCH, LANES), row),
            pl.BlockSpec((tm, ROUTER_PAD), row),
            pl.BlockSpec((tm, ROUTER_PAD), row),
        ],
        out_shape=[
            jax.ShapeDtypeStruct((s, d), F32),
            jax.ShapeDtypeStruct((s * SLAB_PITCH, LANES), U32),
            jax.ShapeDtypeStruct((s, ROUTER_PAD), jnp.int32),
            jax.ShapeDtypeStruct((s, ROUTER_PAD), F32),
        ],
        compiler_params=_cparams(("arbitrary",)),
        name="post_mixer",
    )(ya, yb, z, z, x, wa, wb, wo, g1, ln_g, ln_b, sc2, sh2, rw_pad, rb_pad)


def _slab_copy(src_hbm, dst_vmem, src_row, dst_row, sem):
    return pltpu.make_async_copy(src_hbm.at[pl.ds(src_row, N_SLABS)],
                                 dst_vmem.at[pl.ds(dst_row, N_SLABS)], sem)


def _gather_rows(src_hbm, dst_vmem, idx_ref, first, stop, dst_base, dst_row, sem, unrolled):
    if unrolled:
        for r in range(first, stop):
            _slab_copy(src_hbm, dst_vmem, idx_ref[0, 0, r], dst_base + dst_row(r) * SLAB_PITCH, sem).start()
    else:
        def issue(r, carry):
            _slab_copy(src_hbm, dst_vmem, idx_ref[0, 0, r], dst_base + dst_row(r) * SLAB_PITCH, sem).start()
            return carry

        lax.fori_loop(first, stop, issue, 0)


def _wait_rows(src_hbm, dst_vmem, n, dst_base, sem):
    def drain(r, carry):
        _slab_copy(src_hbm, dst_vmem, 0, dst_base + r * SLAB_PITCH, sem).wait()
        return carry

    lax.fori_loop(0, n, drain, 0, unroll=8)


def _moe_body(layer, te_ref, nv_ref, first_ref, par_ref, nxt_ref, rows0_ref, rows1_ref, rows2_ref, h_hbm,
              wg_hbm, bg_ref, wu_hbm, bu_ref, wd_hbm, bd_ref, y_ref, xbuf, wg32, wu32, wd32, wg16, wu16,
              wd16, sem, wsem):
    i = pl.program_id(0)
    tm = xbuf.shape[0] // (GATHER_SLOTS * SLAB_PITCH)
    slot_rows = tm * SLAB_PITCH
    n_valid = nv_ref[0]
    same_row = lambda r: r

    def weight_copies(expert, wslot):
        row = layer * N_EXPERTS + expert
        return (pltpu.make_async_copy(wg_hbm.at[row], wg32.at[wslot], wsem.at[wslot, 0]),
                pltpu.make_async_copy(wu_hbm.at[row], wu32.at[wslot], wsem.at[wslot, 1]),
                pltpu.make_async_copy(wd_hbm.at[row], wd32.at[wslot], wsem.at[wslot, 2]))

    @pl.when(i == 0)
    def _():
        for cp in weight_copies(te_ref[0], 0):
            cp.start()

    def request(idx_ref, tile, unrolled):
        slot = lax.rem(tile, GATHER_SLOTS)

        @pl.when(tile < n_valid)
        def _():
            _gather_rows(h_hbm, xbuf, idx_ref, 0, tm, slot * slot_rows, same_row, sem.at[slot], unrolled)

    @pl.when(i == 0)
    def _():
        request(rows0_ref, i, False)
        request(rows1_ref, i + 1, False)

    request(rows2_ref, i + 2, True)

    @pl.when(i < n_valid)
    def _():
        @pl.when(first_ref[i] == 1)
        def _():
            wslot = par_ref[i]
            for cp in weight_copies(te_ref[i], wslot):
                cp.wait()
            wg16[...] = wg32[wslot].astype(BF16)
            wu16[...] = wu32[wslot].astype(BF16)
            wd16[...] = wd32[wslot].astype(BF16)

            @pl.when(nxt_ref[i] >= 0)
            def _():
                for cp in weight_copies(nxt_ref[i], 1 - wslot):
                    cp.start()

        slot = lax.rem(i, GATHER_SLOTS)
        base = slot * slot_rows
        _wait_rows(h_hbm, xbuf, tm, base, sem.at[slot])
        halves = [_load_slab(xbuf, base, tm, a) for a in range(N_SLABS)]
        xb = jnp.concatenate([hi.astype(BF16) for hi, _ in halves] +
                             [lo.astype(BF16) for _, lo in halves], axis=1)
        g = jnp.minimum(jnp.dot(xb, wg16[...], preferred_element_type=F32) + bg_ref[0], SWIGLU_LIMIT)
        u = jnp.clip(jnp.dot(xb, wu16[...], preferred_element_type=F32) + bu_ref[0],
                     -SWIGLU_LIMIT, SWIGLU_LIMIT)
        act = (g * jax.nn.sigmoid(SWIGLU_ALPHA * g) * (u + 1.0)).astype(BF16)
        y = jnp.dot(act, wd16[...], preferred_element_type=F32) + bd_ref[0]
        _store_slabs(y_ref, 0, tm, y)

    @pl.when(i >= nv_ref[0])
    def _():
        y_ref[...] = jnp.zeros(y_ref.shape, U32)


def _moe_experts(layer, tile_tables, row_token, h2, wg, bg, wu, bu, wd, bd):
    d = D_MODEL
    n_tiles = row_token.shape[0]
    tm = row_token.shape[2]
    tile_expert, n_valid, run_first, run_parity, next_expert = tile_tables
    bmap = lambda i, te, *_: (layer * N_EXPERTS + te[i], 0, 0)
    hbm = pl.BlockSpec(memory_space=pl.ANY)
    grid_spec = pltpu.PrefetchScalarGridSpec(
        num_scalar_prefetch=5,
        grid=(n_tiles,),
        in_specs=[
            pl.BlockSpec((1, 1, tm), lambda i, *_: (i, 0, 0), memory_space=pltpu.SMEM),
            pl.BlockSpec((1, 1, tm), lambda i, *_: (jnp.minimum(i + 1, n_tiles - 1), 0, 0),
                         memory_space=pltpu.SMEM),
            pl.BlockSpec((1, 1, tm), lambda i, *_: (jnp.minimum(i + 2, n_tiles - 1), 0, 0),
                         memory_space=pltpu.SMEM),
            hbm,
            hbm,
            pl.BlockSpec((1, 1, D_EXPERT), bmap),
            hbm,
            pl.BlockSpec((1, 1, D_EXPERT), bmap),
            hbm,
            pl.BlockSpec((1, 1, d), bmap),
        ],
        out_specs=pl.BlockSpec((tm * SLAB_PITCH, LANES), lambda i, *_: (i, 0)),
        scratch_shapes=[
            pltpu.VMEM((GATHER_SLOTS * tm * SLAB_PITCH, LANES), U32),
            pltpu.VMEM((2, d, D_EXPERT), F32),
            pltpu.VMEM((2, d, D_EXPERT), F32),
            pltpu.VMEM((2, D_EXPERT, d), F32),
            pltpu.VMEM((d, D_EXPERT), BF16),
            pltpu.VMEM((d, D_EXPERT), BF16),
            pltpu.VMEM((D_EXPERT, d), BF16),
            pltpu.SemaphoreType.DMA((GATHER_SLOTS,)),
            pltpu.SemaphoreType.DMA((2, 3)),
        ],
    )
    return pl.pallas_call(
        functools.partial(_moe_body, layer),
        grid_spec=grid_spec,
        out_shape=jax.ShapeDtypeStruct((n_tiles * tm * SLAB_PITCH, LANES), U32),
        compiler_params=_cparams(("arbitrary",)),
        name="moe_experts",
    )(tile_expert, n_valid, run_first, run_parity, next_expert, row_token, row_token, row_token, h2,
      wg, bg, wu, bu, wd, bd)


def _combine_body(pos_ref, pos_next_ref, pos_next2_ref, tw_ref, x_ref, g2_ref, lng_ref, lnb_ref, y_hbm,
                  o_ref, ybuf, sem):
    i = pl.program_id(0)
    tm = x_ref.shape[0]
    n_rows = tm * TOP_K
    slot_rows = n_rows * SLAB_PITCH
    k_major = lambda r: (r % TOP_K) * tm + r // TOP_K

    def request(idx_ref, step, unrolled):
        slot = lax.rem(step, GATHER_SLOTS)

        @pl.when(step < pl.num_programs(0))
        def _():
            _gather_rows(y_hbm, ybuf, idx_ref, 0, n_rows, slot * slot_rows, k_major, sem.at[slot], unrolled)

    @pl.when(i == 0)
    def _():
        request(pos_ref, i, False)
        request(pos_next_ref, i + 1, False)

    request(pos_next2_ref, i + 2, True)

    slot = lax.rem(i, GATHER_SLOTS)
    base = slot * slot_rows
    _wait_rows(y_hbm, ybuf, n_rows, base, sem.at[slot])

    tw = tw_ref[...]
    first, second = [], []
    for a in range(N_SLABS):
        acc_hi = acc_lo = None
        for k in range(TOP_K):
            hi, lo = _load_slab(ybuf, base + k * tm * SLAB_PITCH, tm, a)
            wk = tw[:, k:k + 1]
            acc_hi = wk * hi if acc_hi is None else acc_hi + wk * hi
            acc_lo = wk * lo if acc_lo is None else acc_lo + wk * lo
        first.append(acc_hi)
        second.append(acc_lo)
    ffn = jnp.concatenate(first + second, axis=1)
    o_ref[...] = _layer_norm(DEEPNORM_ALPHA * x_ref[...] + (1.0 + g2_ref[...]) * ffn,
                             lng_ref[...], lnb_ref[...])


def _combine(pos, tw, x1, g2, ln_g, ln_b, y_sorted):
    s, d = x1.shape
    tm = min(TM_COMB, s)
    n_steps = s // tm
    row = lambda i: (i, 0)
    const = lambda i: (0, 0)
    pos3 = pos.reshape(n_steps, 1, tm * TOP_K)
    return pl.pallas_call(
        _combine_body,
        grid=(n_steps,),
        in_specs=[
            pl.BlockSpec((1, 1, tm * TOP_K), lambda i: (i, 0, 0), memory_space=pltpu.SMEM),
            pl.BlockSpec((1, 1, tm * TOP_K), lambda i: (jnp.minimum(i + 1, n_steps - 1), 0, 0),
                         memory_space=pltpu.SMEM),
            pl.BlockSpec((1, 1, tm * TOP_K), lambda i: (jnp.minimum(i + 2, n_steps - 1), 0, 0),
                         memory_space=pltpu.SMEM),
            pl.BlockSpec((tm, ROUTER_PAD), row),
            pl.BlockSpec((tm, d), row),
            pl.BlockSpec((1, d), const),
            pl.BlockSpec((1, d), const),
            pl.BlockSpec((1, d), const),
            pl.BlockSpec(memory_space=pl.ANY),
        ],
        out_specs=pl.BlockSpec((tm, d), row),
        out_shape=jax.ShapeDtypeStruct((s, d), F32),
        scratch_shapes=[pltpu.VMEM((GATHER_SLOTS * TOP_K * tm * SLAB_PITCH, LANES), U32),
                        pltpu.SemaphoreType.DMA((GATHER_SLOTS,))],
        compiler_params=_cparams(("arbitrary",)),
        name="moe_combine",
    )(pos3, pos3, pos3, tw, x1, g2, ln_g, ln_b, y_sorted)


def _routing_tables(top_i, s):
    tm = TM_MOE
    n_pairs = s * TOP_K
    n_tiles = n_pairs // tm + N_EXPERTS
    flat_e = top_i.reshape(n_pairs)
    onehot = (flat_e[:, None] == jnp.arange(N_EXPERTS, dtype=jnp.int32)[None, :]).astype(jnp.int32)
    csum = jnp.cumsum(onehot, axis=0)
    rank = jnp.sum(csum * onehot, axis=1) - 1
    counts = csum[-1]
    padded = ((counts + tm - 1) // tm) * tm
    ends = jnp.cumsum(padded)
    starts = ends - padded
    pos = jnp.sum(onehot * starts[None, :], axis=1) + rank
    row_token = jnp.zeros((n_tiles * tm,), jnp.int32).at[pos].set(
        (jnp.arange(n_pairs, dtype=jnp.int32) // TOP_K) * SLAB_PITCH,
        unique_indices=True, mode="promise_in_bounds")
    tile_start = jnp.arange(n_tiles, dtype=jnp.int32) * tm
    tile_expert = jnp.sum((tile_start[:, None] >= ends[None, :]).astype(jnp.int32), axis=1)
    tile_expert = jnp.minimum(tile_expert, N_EXPERTS - 1).astype(jnp.int32)
    n_valid = (ends[-1] // tm).astype(jnp.int32).reshape(1)
    run_first = jnp.concatenate([jnp.ones((1,), jnp.int32),
                                 (tile_expert[1:] != tile_expert[:-1]).astype(jnp.int32)])
    run_parity = (jnp.cumsum(run_first) - 1) % 2
    experts = jnp.arange(N_EXPERTS, dtype=jnp.int32)
    present = jnp.where(counts > 0, experts, N_EXPERTS)
    later = jnp.concatenate([lax.cummin(present[::-1])[::-1][1:], jnp.full((1,), N_EXPERTS, jnp.int32)])
    next_expert = jnp.where(later < N_EXPERTS, later, -1)[tile_expert]
    pos_flat = (pos * SLAB_PITCH).astype(jnp.int32)
    tile_tables = (tile_expert, n_valid, run_first.astype(jnp.int32), run_parity.astype(jnp.int32),
                   next_expert.astype(jnp.int32))
    return pos_flat, row_token.reshape(n_tiles, 1, tm), tile_tables


def kernel(x, c, w_ada, b_ada, w_in, b_f, sgu_ln_g, spatial_w, spatial_b, w_proj_a, w_proj_b, w_o,
           ln1_g, ln1_b, router_w, router_b, w_gate, b_gate, w_up, b_up, w_down, b_down, ln2_g, ln2_b):
    bsz, s, d = x.shape
    assert bsz == 1
    depth = w_ada.shape[0]
    xs = x.reshape(s, d)
    mod = _adaln(c.reshape(d, 1), w_ada, b_ada)

    n_a = 2 * A_WIDTH
    n_b = 3 * B_WIDTH
    w_main = jnp.concatenate(
        [w_in[:, :, n_a + n_b + B_HEADS:], w_in[:, :, :n_a + n_b]], axis=-1).astype(BF16)

    def split_pad(w):
        hi = w.astype(BF16)
        lo = (w - hi.astype(F32)).astype(BF16)
        pad = ((0, 0), (0, 0), (0, LANES - w.shape[-1]))
        return jnp.concatenate([jnp.pad(hi, pad), jnp.pad(lo, pad)], axis=-1)

    w_f = split_pad(w_in[:, :, n_a + n_b:n_a + n_b + B_HEADS])
    wa16 = w_proj_a.astype(BF16)
    wb16 = w_proj_b.astype(BF16)
    wo16 = w_o.astype(BF16)
    rw_pad = split_pad(router_w)
    rb_pad = jnp.pad(router_b, ((0, 0), (0, ROUTER_PAD - N_EXPERTS)), constant_values=float("-inf"))

    n_all = depth * N_EXPERTS
    wg_all = w_gate.reshape(n_all, d, D_EXPERT)
    wu_all = w_up.reshape(n_all, d, D_EXPERT)
    wd_all = w_down.reshape(n_all, D_EXPERT, d)
    bg_all = b_gate.reshape(n_all, 1, D_EXPERT)
    bu_all = b_up.reshape(n_all, 1, D_EXPERT)
    bd_all = b_down.reshape(n_all, 1, d)

    for l in range(depth):
        sh1, sc1, g1, sh2, sc2, g2 = [mod[l, :, k * d:(k + 1) * d] for k in range(6)]
        z, f_logit = _inproj(xs, sc1, sh1, w_main[l], w_f[l])
        q_sumsq, k_sumsq = _tile_norms(z)
        cum_hs = _forget_cumsum(f_logit.T, b_f[l].reshape(B_HEADS, 1))
        ya = _sgu(z, sgu_ln_g[l].reshape(1, A_WIDTH), spatial_w[l], spatial_b[l].T)
        yb = _attention(z, cum_hs, _prune_bounds(q_sumsq, k_sumsq, cum_hs))
        x1, h2, top_i, top_w = _post_mixer(
            ya, yb, z, xs, wa16[l], wb16[l], wo16[l], g1, ln1_g[l].reshape(1, d),
            ln1_b[l].reshape(1, d), sc2, sh2, rw_pad[l], rb_pad[l].reshape(1, ROUTER_PAD))
        pos, row_token, tile_tables = _routing_tables(top_i[:, :TOP_K], s)
        y_sorted = _moe_experts(l, tile_tables, row_token, h2, wg_all, bg_all, wu_all, bu_all,
                                wd_all, bd_all)
        xs = _combine(pos, top_w, x1, g2, ln2_g[l].reshape(1, d), ln2_b[l].reshape(1, d), y_sorted)
    return xs.reshape(bsz, s, d)
```

```python
import functools

import jax
import jax.numpy as jnp
from jax import lax
from jax.experimental import pallas as pl
from jax.experimental.pallas import tpu as pltpu

F32 = jnp.float32
BF16 = jnp.bfloat16
HIGHEST = lax.Precision.HIGHEST

D_MODEL = 2048
DEPTH = 4
CHUNK = 128
A_GROUPS = 8
A_WIDTH = 1024
B_HEADS = 8
B_HEAD_DIM = 128
B_WIDTH = 1024
N_EXPERTS = 32
TOP_K = 4
D_EXPERT = 512
SWIGLU_LIMIT = 7.0
SWIGLU_ALPHA = 1.702
DEEPNORM_ALPHA = (2.0 * DEPTH) ** 0.25
LN_EPS = 1e-5
MASK_VALUE = -1e30

LANES = 128
VMEM_LIMIT = 56 * 1024 * 1024

Z_COLS = 2 * D_MODEL + 2 * A_WIDTH + 3 * B_WIDTH
Z_GA, Z_GB, Z_U, Z_V, Z_Q, Z_K, Z_VB = 0, 2048, 4096, 5120, 6144, 7168, 8192

TN_IN = 1024
TN_SUB = 256
TM_IN = 1024
TM_SGU = 256
TQ = 512
PRUNE_MARGIN = 110.0
NORM_SLACK = 1.01
ATTN_ROW_GROUPS = 2
TM_POST = 256
TM_MOE = 256
TM_COMB = 64
CUM_CHUNK = 512
ROUTER_PAD = 128

U32 = jnp.uint32
N_SLABS = D_MODEL // (2 * LANES)
SLAB_PITCH = 9
GATHER_SLOTS = 3


def _bf16_bits(v):
    return lax.bitcast_convert_type(v.astype(BF16).astype(F32), U32)


def _store_slabs(ref, base, rows, value):
    for a in range(N_SLABS):
        hi = _bf16_bits(value[:, a * LANES:(a + 1) * LANES])
        lo = _bf16_bits(value[:, (a + N_SLABS) * LANES:(a + N_SLABS + 1) * LANES])
        ref[pl.ds(base + a, rows, stride=SLAB_PITCH), :] = hi | (lo >> 16)
    zeros = jnp.zeros((rows, LANES), U32)
    for a in range(N_SLABS, SLAB_PITCH):
        ref[pl.ds(base + a, rows, stride=SLAB_PITCH), :] = zeros


def _load_slab(ref, base, rows, a):
    w = ref[pl.ds(base + a, rows, stride=SLAB_PITCH), :]
    hi = lax.bitcast_convert_type(w & jnp.uint32(0xFFFF0000), F32)
    lo = lax.bitcast_convert_type(w << 16, F32)
    return hi, lo


def _split_bf16(v):
    hi = v.astype(BF16)
    lo = (v - hi.astype(F32)).astype(BF16)
    return hi, lo


def _cparams(sem, vmem=VMEM_LIMIT):
    return pltpu.CompilerParams(dimension_semantics=sem, vmem_limit_bytes=vmem)


def _adaln_body(c_ref, w_ref, b_ref, o_ref):
    c = c_ref[...]
    cond = c * jax.nn.sigmoid(c)
    o_ref[0] = jnp.sum(w_ref[0] * cond, axis=0, keepdims=True) + b_ref[0]


def _adaln(c_col, w_ada, b_ada):
    depth, d, n = w_ada.shape
    tn = 1024
    return pl.pallas_call(
        _adaln_body,
        grid=(depth, n // tn),
        in_specs=[
            pl.BlockSpec((d, 1), lambda l, j: (0, 0)),
            pl.BlockSpec((1, d, tn), lambda l, j: (l, 0, j)),
            pl.BlockSpec((1, 1, tn), lambda l, j: (l, 0, j)),
        ],
        out_specs=pl.BlockSpec((1, 1, tn), lambda l, j: (l, 0, j)),
        out_shape=jax.ShapeDtypeStruct((depth, 1, n), F32),
        compiler_params=_cparams(("arbitrary", "arbitrary")),
        name="adaln",
    )(c_col, w_ada, b_ada.reshape(depth, 1, n))


def _gelu_tanh(x):
    return 0.5 * x * (1.0 + jnp.tanh(0.7978845608028654 * (x + 0.044715 * (x * x * x))))


def _inproj_body(x_ref, sc_ref, sh_ref, w_ref, wf_ref, z_ref, f_ref, h_scr):
    j = pl.program_id(1)

    @pl.when(j == 0)
    def _():
        h = x_ref[...] * (1.0 + sc_ref[...]) + sh_ref[...]
        h_hi, h_lo = _split_bf16(h)
        h_scr[...] = h_hi
        r1 = jnp.dot(h_hi, wf_ref[...], preferred_element_type=F32)
        r2 = jnp.dot(h_lo, wf_ref[:, :LANES], preferred_element_type=F32)
        f_ref[...] = (r1[:, :LANES] + r1[:, LANES:] + r2)[:, :B_HEADS]

    n_sig = (2 * D_MODEL) // TN_IN
    n_gelu = (2 * A_WIDTH) // TN_IN
    n_sub = TN_IN // TN_SUB

    def project(epilogue):
        for c in range(n_sub):
            cols = slice(c * TN_SUB, (c + 1) * TN_SUB)
            acc = jnp.dot(h_scr[...], w_ref[:, cols], preferred_element_type=F32)
            z_ref[:, cols] = epilogue(acc).astype(BF16)

    @pl.when(j < n_sig)
    def _():
        project(jax.nn.sigmoid)

    @pl.when(jnp.logical_and(j >= n_sig, j < n_sig + n_gelu))
    def _():
        project(_gelu_tanh)

    @pl.when(j >= n_sig + n_gelu)
    def _():
        project(lambda a: a)


def _inproj(x, sc, sh, w_main, w_f):
    s, d = x.shape
    tm = min(TM_IN, s)
    return pl.pallas_call(
        _inproj_body,
        grid=(s // tm, Z_COLS // TN_IN),
        in_specs=[
            pl.BlockSpec((tm, d), lambda i, j: (i, 0)),
            pl.BlockSpec((1, d), lambda i, j: (0, 0)),
            pl.BlockSpec((1, d), lambda i, j: (0, 0)),
            pl.BlockSpec((d, TN_IN), lambda i, j: (0, j)),
            pl.BlockSpec((d, 2 * LANES), lambda i, j: (0, 0)),
        ],
        out_specs=[
            pl.BlockSpec((tm, TN_IN), lambda i, j: (i, j)),
            pl.BlockSpec((tm, B_HEADS), lambda i, j: (i, 0)),
        ],
        out_shape=[
            jax.ShapeDtypeStruct((s, Z_COLS), BF16),
            jax.ShapeDtypeStruct((s, B_HEADS), F32),
        ],
        scratch_shapes=[pltpu.VMEM((tm, d), BF16)],
        compiler_params=_cparams(("arbitrary", "arbitrary")),
        name="inproj",
    )(x, sc, sh, w_main, w_f)


def _cum_body(f_ref, bf_ref, o_ref):
    n_chunks = f_ref.shape[1] // CUM_CHUNK
    row = lax.broadcasted_iota(jnp.int32, (CUM_CHUNK, CUM_CHUNK), 0)
    col = lax.broadcasted_iota(jnp.int32, (CUM_CHUNK, CUM_CHUNK), 1)
    upper = jnp.where(row <= col, 1.0, 0.0).astype(F32)

    def body(i, carry):
        sl = pl.ds(pl.multiple_of(i * CUM_CHUNK, CUM_CHUNK), CUM_CHUNK)
        xf = f_ref[:, sl] + bf_ref[...]
        logf = jnp.minimum(xf, 0.0) - jnp.log1p(jnp.exp(-jnp.abs(xf)))
        cs = jnp.dot(logf, upper, precision=HIGHEST, preferred_element_type=F32) + carry
        o_ref[:, sl] = cs
        return cs[:, CUM_CHUNK - 1:CUM_CHUNK]

    lax.fori_loop(0, n_chunks, body, jnp.zeros((B_HEADS, 1), F32))


def _forget_cumsum(f_t, b_f_col):
    h, s = f_t.shape
    return pl.pallas_call(
        _cum_body,
        out_shape=jax.ShapeDtypeStruct((h, s), F32),
        compiler_params=pltpu.CompilerParams(vmem_limit_bytes=VMEM_LIMIT),
        name="forget_cumsum",
    )(f_t, b_f_col)


def _sgu_body(u_ref, v_ref, g_ref, w_ref, b_ref, o_ref):
    tm = u_ref.shape[0]
    row = lax.broadcasted_iota(jnp.int32, (CHUNK, CHUNK), 0)
    col = lax.broadcasted_iota(jnp.int32, (CHUNK, CHUNK), 1)
    causal = row >= col
    for g in range(A_GROUPS):
        cols = slice(g * CHUNK, (g + 1) * CHUNK)
        w = jnp.where(causal, w_ref[g], 0.0).astype(BF16)
        bias = b_ref[:, g:g + 1]
        gain = g_ref[:, cols]
        for ch in range(tm // CHUNK):
            rows = slice(ch * CHUNK, (ch + 1) * CHUNK)
            v = v_ref[rows, cols].astype(F32)
            mu = jnp.mean(v, axis=-1, keepdims=True)
            vc = v - mu
            var = jnp.mean(vc * vc, axis=-1, keepdims=True)
            vn = vc * lax.rsqrt(var + LN_EPS) * gain
            mixed = jnp.dot(w, vn.astype(BF16), preferred_element_type=F32) + bias
            o_ref[rows, cols] = (u_ref[rows, cols].astype(F32) * mixed).astype(BF16)


def _sgu(z, ln_g, w_s, b_s_t):
    s = z.shape[0]
    tm = min(TM_SGU, s)
    return pl.pallas_call(
        _sgu_body,
        grid=(s // tm,),
        in_specs=[
            pl.BlockSpec((tm, A_WIDTH), lambda i: (i, Z_U // A_WIDTH)),
            pl.BlockSpec((tm, A_WIDTH), lambda i: (i, Z_V // A_WIDTH)),
            pl.BlockSpec((1, A_WIDTH), lambda i: (0, 0)),
            pl.BlockSpec((A_GROUPS, CHUNK, CHUNK), lambda i: (0, 0, 0)),
            pl.BlockSpec((CHUNK, A_GROUPS), lambda i: (0, 0)),
        ],
        out_specs=pl.BlockSpec((tm, A_WIDTH), lambda i: (i, 0)),
        out_shape=jax.ShapeDtypeStruct((s, A_WIDTH), BF16),
        compiler_params=_cparams(("arbitrary",)),
        name="sgu",
    )(z, z, ln_g, w_s, b_s_t)


def _attn_body(lo_ref, q_ref, k_ref, v_ref, ck_ref, o_ref, m_scr, acc_scr, s_even, s_odd):
    h = pl.program_id(0)
    i = pl.program_id(1)
    tq = q_ref.shape[0]
    tk = tq
    scale = B_HEAD_DIM ** -0.5
    q = q_ref[...]
    c_ref = ck_ref[0, :, pl.ds(pl.multiple_of(i * tq, tq), LANES)][:, 0:1]
    ones = jnp.ones((tk, B_HEAD_DIM), BF16)
    lo = lo_ref[h, i]

    m_scr[...] = jnp.full((tq, 1), MASK_VALUE, F32)
    acc_scr[...] = jnp.zeros((tq, 2 * B_HEAD_DIM), F32)

    def key_slice(j):
        return pl.ds(pl.multiple_of(j * tk, tk), tk)

    def raw_scores(j):
        return lax.dot_general(q, k_ref[key_slice(j), :], (((1,), (1,)), ((), ())),
                               preferred_element_type=F32)

    def consume(s_ref, j, masked):
        ks = key_slice(j)
        v_aug = jnp.concatenate([v_ref[ks, :], ones], axis=1)
        bias = c_ref - ck_ref[0, :, ks]
        rows_per_group = tq // ATTN_ROW_GROUPS
        for g in range(ATTN_ROW_GROUPS):
            rows = slice(g * rows_per_group, (g + 1) * rows_per_group)
            s = s_ref[rows, :] * scale + bias
            if masked:
                r = lax.broadcasted_iota(jnp.int32, (rows_per_group, tk), 0) + g * rows_per_group
                c = lax.broadcasted_iota(jnp.int32, (rows_per_group, tk), 1)
                s = jnp.where(r >= c, s, MASK_VALUE)
            m_prev = m_scr[rows, :]
            m_new = jnp.maximum(m_prev, jnp.max(s, axis=1, keepdims=True))
            p = jnp.exp(s - m_new)
            alpha = jnp.exp(m_prev - m_new)
            acc_scr[rows, :] = alpha * acc_scr[rows, :] + jnp.dot(
                p.astype(BF16), v_aug, preferred_element_type=F32)
            m_scr[rows, :] = m_new

    s_even[...] = raw_scores(lo)

    def body(j, carry):
        even_visit = lax.rem(j - lo, 2) == 0

        @pl.when(even_visit)
        def _():
            s_odd[...] = raw_scores(j + 1)
            consume(s_even, j, False)

        @pl.when(jnp.logical_not(even_visit))
        def _():
            s_even[...] = raw_scores(j + 1)
            consume(s_odd, j, False)

        return carry

    lax.fori_loop(lo, i, body, 0)
    last_even = lax.rem(i - lo, 2) == 0

    @pl.when(last_even)
    def _():
        consume(s_even, i, True)

    @pl.when(jnp.logical_not(last_even))
    def _():
        consume(s_odd, i, True)

    acc = acc_scr[...]
    o_ref[...] = (acc[:, :B_HEAD_DIM] / acc[:, B_HEAD_DIM:]).astype(BF16)


def _attention(z, cum_hs, lo):
    s = z.shape[0]
    tq = min(TQ, s)
    grid_spec = pltpu.PrefetchScalarGridSpec(
        num_scalar_prefetch=1,
        grid=(B_HEADS, s // tq),
        in_specs=[
            pl.BlockSpec((tq, B_HEAD_DIM), lambda h, i, lo: (i, Z_Q // B_HEAD_DIM + h)),
            pl.BlockSpec((s, B_HEAD_DIM), lambda h, i, lo: (0, Z_K // B_HEAD_DIM + h)),
            pl.BlockSpec((s, B_HEAD_DIM), lambda h, i, lo: (0, Z_VB // B_HEAD_DIM + h)),
            pl.BlockSpec((1, 1, s), lambda h, i, lo: (h, 0, 0)),
        ],
        out_specs=pl.BlockSpec((tq, B_HEAD_DIM), lambda h, i, lo: (i, h)),
        scratch_shapes=[
            pltpu.VMEM((tq, 1), F32),
            pltpu.VMEM((tq, 2 * B_HEAD_DIM), F32),
            pltpu.VMEM((tq, tq), F32),
            pltpu.VMEM((tq, tq), F32),
        ],
    )
    return pl.pallas_call(
        _attn_body,
        grid_spec=grid_spec,
        out_shape=jax.ShapeDtypeStruct((s, B_WIDTH), BF16),
        compiler_params=_cparams(("arbitrary", "arbitrary")),
        name="fox_attention",
    )(lo, z, z, z, cum_hs.reshape(B_HEADS, 1, s))


def _tile_norm_body(q_ref, k_ref, qo_ref, ko_ref):
    col = lax.broadcasted_iota(jnp.int32, (B_WIDTH, LANES), 0)
    head = lax.broadcasted_iota(jnp.int32, (B_WIDTH, LANES), 1)
    sel = jnp.where(col // B_HEAD_DIM == head, 1.0, 0.0).astype(BF16)
    for src, dst in ((q_ref, qo_ref), (k_ref, ko_ref)):
        zc = src[...]
        sumsq = jnp.dot(zc * zc, sel, preferred_element_type=F32)
        dst[...] = jnp.broadcast_to(jnp.max(sumsq, axis=0, keepdims=True), dst.shape)


def _tile_norms(z):
    s = z.shape[0]
    tq = min(TQ, s)
    nblk = s // tq
    sub = 8
    out = jax.ShapeDtypeStruct((nblk * sub, LANES), F32)
    q_max, k_max = pl.pallas_call(
        _tile_norm_body,
        grid=(nblk,),
        in_specs=[
            pl.BlockSpec((tq, B_WIDTH), lambda i: (i, Z_Q // B_WIDTH)),
            pl.BlockSpec((tq, B_WIDTH), lambda i: (i, Z_K // B_WIDTH)),
        ],
        out_specs=[pl.BlockSpec((sub, LANES), lambda i: (i, 0)),
                   pl.BlockSpec((sub, LANES), lambda i: (i, 0))],
        out_shape=[out, out],
        compiler_params=_cparams(("arbitrary",)),
        name="tile_norms",
    )(z, z)
    pick = lambda t: t.reshape(nblk, sub, LANES)[:, 0, :B_HEADS]
    return pick(q_max), pick(k_max)


def _prune_bounds(q_sumsq, k_sumsq, cum_hs):
    s = cum_hs.shape[1]
    tq = min(TQ, s)
    nblk = s // tq
    scale = B_HEAD_DIM ** -0.5

    qn = jnp.sqrt(q_sumsq).T
    kn = jnp.sqrt(k_sumsq).T
    kpm = lax.cummax(kn, axis=1)
    c_first = cum_hs[:, ::tq]
    c_last = cum_hs[:, tq - 1::tq]
    dot_bound = scale * qn[:, :, None] * (kpm[:, None, :] + kn[:, :, None]) * NORM_SLACK
    bound = dot_bound + c_first[:, :, None] - c_last[:, None, :]
    tile = jnp.arange(nblk, dtype=jnp.int32)
    skip = jnp.logical_and(bound <= -PRUNE_MARGIN, tile[None, None, :] < tile[None, :, None])
    return jnp.sum(skip.astype(jnp.int32), axis=-1)


def _layer_norm(y, g, b):
    mu = jnp.mean(y, axis=-1, keepdims=True)
    yc = y - mu
    var = jnp.mean(yc * yc, axis=-1, keepdims=True)
    return yc * lax.rsqrt(var + LN_EPS) * g + b


def _post_body(ya_ref, yb_ref, ga_ref, gb_ref, x_ref, wa_ref, wb_ref, wo_ref, g1_ref, lng_ref,
               lnb_ref, sc_ref, sh_ref, rw_ref, rb_ref, x1_ref, h2_ref, ti_ref, tw_ref):
    tm = x_ref.shape[0]
    a = jnp.dot(ya_ref[...], wa_ref[...], preferred_element_type=F32)
    b = jnp.dot(yb_ref[...], wb_ref[...], preferred_element_type=F32)
    merged = ga_ref[...].astype(F32) * a + gb_ref[...].astype(F32) * b
    mix = jnp.dot(merged.astype(BF16), wo_ref[...], preferred_element_type=F32)
    x1 = _layer_norm(DEEPNORM_ALPHA * x_ref[...] + (1.0 + g1_ref[...]) * mix,
                     lng_ref[...], lnb_ref[...])
    x1_ref[...] = x1
    h2 = x1 * (1.0 + sc_ref[...]) + sh_ref[...]
    _store_slabs(h2_ref, 0, tm, h2)
    h_hi, h_lo = _split_bf16(h2)
    r1 = jnp.dot(h_hi, rw_ref[...], preferred_element_type=F32)
    r2 = jnp.dot(h_lo, rw_ref[:, :ROUTER_PAD], preferred_element_type=F32)
    logits = r1[:, :ROUTER_PAD] + r1[:, ROUTER_PAD:] + r2 + rb_ref[...]
    lane = lax.broadcasted_iota(jnp.int32, (tm, ROUTER_PAD), 1)
    lane_f = lane.astype(F32)
    neg_inf = float("-inf")
    work = logits
    vals, idxs = [], []
    for _ in range(TOP_K):
        m = jnp.max(work, axis=1, keepdims=True)
        idx = jnp.min(jnp.where(work == m, lane_f, float(ROUTER_PAD)), axis=1, keepdims=True)
        vals.append(m)
        idxs.append(idx)
        work = jnp.where(lane_f == idx, neg_inf, work)
    exps = [jnp.exp(v - vals[0]) for v in vals]
    denom = exps[0] + exps[1] + exps[2] + exps[3]
    ti = jnp.zeros((tm, ROUTER_PAD), F32)
    tw = jnp.zeros((tm, ROUTER_PAD), F32)
    for k in range(TOP_K):
        ti = jnp.where(lane == k, idxs[k], ti)
        tw = jnp.where(lane == k, exps[k] / denom, tw)
    ti_ref[...] = ti.astype(jnp.int32)
    tw_ref[...] = tw


def _post_mixer(ya, yb, z, x, wa, wb, wo, g1, ln_g, ln_b, sc2, sh2, rw_pad, rb_pad):
    s, d = x.shape
    tm = min(TM_POST, s)
    row = lambda i: (i, 0)
    const = lambda i: (0, 0)
    single = pl.Buffered(1)
    return pl.pallas_call(
        _post_body,
        grid=(s // tm,),
        in_specs=[
            pl.BlockSpec((tm, A_WIDTH), row),
            pl.BlockSpec((tm, B_WIDTH), row),
            pl.BlockSpec((tm, d), lambda i: (i, Z_GA // D_MODEL)),
            pl.BlockSpec((tm, d), lambda i: (i, Z_GB // D_MODEL)),
            pl.BlockSpec((tm, d), row),
            pl.BlockSpec((A_WIDTH, d), const, pipeline_mode=single),
            pl.BlockSpec((B_WIDTH, d), const, pipeline_mode=single),
            pl.BlockSpec((d, d), const, pipeline_mode=single),
            pl.BlockSpec((1, d), const),
            pl.BlockSpec((1, d), const),
            pl.BlockSpec((1, d), const),
            pl.BlockSpec((1, d), const),
            pl.BlockSpec((1, d), const),
            pl.BlockSpec((d, 2 * ROUTER_PAD), const),
            pl.BlockSpec((1, ROUTER_PAD), const),
        ],
        out_specs=[
            pl.BlockSpec((tm, d), row),
            pl.BlockSpec((tm * SLAB_PITCH, LANES), row),
            pl.BlockSpec((tm, ROUTER_PAD), row),
            pl.BlockSpec((tm, ROUTER_PAD), row),
        ],
        out_shape=[
            jax.ShapeDtypeStruct((s, d), F32),
            jax.ShapeDtypeStruct((s * SLAB_PITCH, LANES), U32),
            jax.ShapeDtypeStruct((s, ROUTER_PAD), jnp.int32),
            jax.ShapeDtypeStruct((s, ROUTER_PAD), F32),
        ],
        compiler_params=_cparams(("arbitrary",)),
        name="post_mixer",
    )(ya, yb, z, z, x, wa, wb, wo, g1, ln_g, ln_b, sc2, sh2, rw_pad, rb_pad)


def _slab_copy(src_hbm, dst_vmem, src_row, dst_row, sem):
    return pltpu.make_async_copy(src_hbm.at[pl.ds(src_row, N_SLABS)],
                                 dst_vmem.at[pl.ds(dst_row, N_SLABS)], sem)


def _gather_rows(src_hbm, dst_vmem, idx_ref, first, stop, dst_base, dst_row, sem, unrolled):
    if unrolled:
        for r in range(first, stop):
            _slab_copy(src_hbm, dst_vmem, idx_ref[0, 0, r], dst_base + dst_row(r) * SLAB_PITCH, sem).start()
    else:
        def issue(r, carry):
            _slab_copy(src_hbm, dst_vmem, idx_ref[0, 0, r], dst_base + dst_row(r) * SLAB_PITCH, sem).start()
            return carry

        lax.fori_loop(first, stop, issue, 0)


def _wait_rows(src_hbm, dst_vmem, n, dst_base, sem):
    def drain(r, carry):
        _slab_copy(src_hbm, dst_vmem, 0, dst_base + r * SLAB_PITCH, sem).wait()
        return carry

    lax.fori_loop(0, n, drain, 0, unroll=8)


def _moe_body(layer, te_ref, nv_ref, first_ref, par_ref, nxt_ref, rows0_ref, rows1_ref, rows2_ref, h_hbm,
              wg_hbm, bg_ref, wu_hbm, bu_ref, wd_hbm, bd_ref, y_ref, xbuf, wg32, wu32, wd32, wg16, wu16,
              wd16, sem, wsem):
    i = pl.program_id(0)
    tm = xbuf.shape[0] // (GATHER_SLOTS * SLAB_PITCH)
    slot_rows = tm * SLAB_PITCH
    n_valid = nv_ref[0]
    same_row = lambda r: r

    def weight_copies(expert, wslot):
        row = layer * N_EXPERTS + expert
        return (pltpu.make_async_copy(wg_hbm.at[row], wg32.at[wslot], wsem.at[wslot, 0]),
                pltpu.make_async_copy(wu_hbm.at[row], wu32.at[wslot], wsem.at[wslot, 1]),
                pltpu.make_async_copy(wd_hbm.at[row], wd32.at[wslot], wsem.at[wslot, 2]))

    @pl.when(i == 0)
    def _():
        for cp in weight_copies(te_ref[0], 0):
            cp.start()

    def request(idx_ref, tile, unrolled):
        slot = lax.rem(tile, GATHER_SLOTS)

        @pl.when(tile < n_valid)
        def _():
            _gather_rows(h_hbm, xbuf, idx_ref, 0, tm, slot * slot_rows, same_row, sem.at[slot], unrolled)

    @pl.when(i == 0)
    def _():
        request(rows0_ref, i, False)
        request(rows1_ref, i + 1, False)

    request(rows2_ref, i + 2, True)

    @pl.when(i < n_valid)
    def _():
        @pl.when(first_ref[i] == 1)
        def _():
            wslot = par_ref[i]
            for cp in weight_copies(te_ref[i], wslot):
                cp.wait()
            wg16[...] = wg32[wslot].astype(BF16)
            wu16[...] = wu32[wslot].astype(BF16)
            wd16[...] = wd32[wslot].astype(BF16)

            @pl.when(nxt_ref[i] >= 0)
            def _():
                for cp in weight_copies(nxt_ref[i], 1 - wslot):
                    cp.start()

        slot = lax.rem(i, GATHER_SLOTS)
        base = slot * slot_rows
        _wait_rows(h_hbm, xbuf, tm, base, sem.at[slot])
        halves = [_load_slab(xbuf, base, tm, a) for a in range(N_SLABS)]
        xb = jnp.concatenate([hi.astype(BF16) for hi, _ in halves] +
                             [lo.astype(BF16) for _, lo in halves], axis=1)
        g = jnp.minimum(jnp.dot(xb, wg16[...], preferred_element_type=F32) + bg_ref[0], SWIGLU_LIMIT)
        u = jnp.clip(jnp.dot(xb, wu16[...], preferred_element_type=F32) + bu_ref[0],
                     -SWIGLU_LIMIT, SWIGLU_LIMIT)
        act = (g * jax.nn.sigmoid(SWIGLU_ALPHA * g) * (u + 1.0)).astype(BF16)
        y = jnp.dot(act, wd16[...], preferred_element_type=F32) + bd_ref[0]
        _store_slabs(y_ref, 0, tm, y)

    @pl.when(i >= nv_ref[0])
    def _():
        y_ref[...] = jnp.zeros(y_ref.shape, U32)


def _moe_experts(layer, tile_tables, row_token, h2, wg, bg, wu, bu, wd, bd):
    d = D_MODEL
    n_tiles = row_token.shape[0]
    tm = row_token.shape[2]
    tile_expert, n_valid, run_first, run_parity, next_expert = tile_tables
    bmap = lambda i, te, *_: (layer * N_EXPERTS + te[i], 0, 0)
    hbm = pl.BlockSpec(memory_space=pl.ANY)
    grid_spec = pltpu.PrefetchScalarGridSpec(
        num_scalar_prefetch=5,
        grid=(n_tiles,),
        in_specs=[
            pl.BlockSpec((1, 1, tm), lambda i, *_: (i, 0, 0), memory_space=pltpu.SMEM),
            pl.BlockSpec((1, 1, tm), lambda i, *_: (jnp.minimum(i + 1, n_tiles - 1), 0, 0),
                         memory_space=pltpu.SMEM),
            pl.BlockSpec((1, 1, tm), lambda i, *_: (jnp.minimum(i + 2, n_tiles - 1), 0, 0),
                         memory_space=pltpu.SMEM),
            hbm,
            hbm,
            pl.BlockSpec((1, 1, D_EXPERT), bmap),
            hbm,
            pl.BlockSpec((1, 1, D_EXPERT), bmap),
            hbm,
            pl.BlockSpec((1, 1, d), bmap),
        ],
        out_specs=pl.BlockSpec((tm * SLAB_PITCH, LANES), lambda i, *_: (i, 0)),
        scratch_shapes=[
            pltpu.VMEM((GATHER_SLOTS * tm * SLAB_PITCH, LANES), U32),
            pltpu.VMEM((2, d, D_EXPERT), F32),
            pltpu.VMEM((2, d, D_EXPERT), F32),
            pltpu.VMEM((2, D_EXPERT, d), F32),
            pltpu.VMEM((d, D_EXPERT), BF16),
            pltpu.VMEM((d, D_EXPERT), BF16),
            pltpu.VMEM((D_EXPERT, d), BF16),
            pltpu.SemaphoreType.DMA((GATHER_SLOTS,)),
            pltpu.SemaphoreType.DMA((2, 3)),
        ],
    )
    return pl.pallas_call(
        functools.partial(_moe_body, layer),
        grid_spec=grid_spec,
        out_shape=jax.ShapeDtypeStruct((n_tiles * tm * SLAB_PITCH, LANES), U32),
        compiler_params=_cparams(("arbitrary",)),
        name="moe_experts",
    )(tile_expert, n_valid, run_first, run_parity, next_expert, row_token, row_token, row_token, h2,
      wg, bg, wu, bu, wd, bd)


def _combine_body(pos_ref, pos_next_ref, pos_next2_ref, tw_ref, x_ref, g2_ref, lng_ref, lnb_ref, y_hbm,
                  o_ref, ybuf, sem):
    i = pl.program_id(0)
    tm = x_ref.shape[0]
    n_rows = tm * TOP_K
    slot_rows = n_rows * SLAB_PITCH
    k_major = lambda r: (r % TOP_K) * tm + r // TOP_K

    def request(idx_ref, step, unrolled):
        slot = lax.rem(step, GATHER_SLOTS)

        @pl.when(step < pl.num_programs(0))
        def _():
            _gather_rows(y_hbm, ybuf, idx_ref, 0, n_rows, slot * slot_rows, k_major, sem.at[slot], unrolled)

    @pl.when(i == 0)
    def _():
        request(pos_ref, i, False)
        request(pos_next_ref, i + 1, False)

    request(pos_next2_ref, i + 2, True)

    slot = lax.rem(i, GATHER_SLOTS)
    base = slot * slot_rows
    _wait_rows(y_hbm, ybuf, n_rows, base, sem.at[slot])

    tw = tw_ref[...]
    first, second = [], []
    for a in range(N_SLABS):
        acc_hi = acc_lo = None
        for k in range(TOP_K):
            hi, lo = _load_slab(ybuf, base + k * tm * SLAB_PITCH, tm, a)
            wk = tw[:, k:k + 1]
            acc_hi = wk * hi if acc_hi is None else acc_hi + wk * hi
            acc_lo = wk * lo if acc_lo is None else acc_lo + wk * lo
        first.append(acc_hi)
        second.append(acc_lo)
    ffn = jnp.concatenate(first + second, axis=1)
    o_ref[...] = _layer_norm(DEEPNORM_ALPHA * x_ref[...] + (1.0 + g2_ref[...]) * ffn,
                             lng_ref[...], lnb_ref[...])


def _combine(pos, tw, x1, g2, ln_g, ln_b, y_sorted):
    s, d = x1.shape
    tm = min(TM_COMB, s)
    n_steps = s // tm
    row = lambda i: (i, 0)
    const = lambda i: (0, 0)
    pos3 = pos.reshape(n_steps, 1, tm * TOP_K)
    return pl.pallas_call(
        _combine_body,
        grid=(n_steps,),
        in_specs=[
            pl.BlockSpec((1, 1, tm * TOP_K), lambda i: (i, 0, 0), memory_space=pltpu.SMEM),
            pl.BlockSpec((1, 1, tm * TOP_K), lambda i: (jnp.minimum(i + 1, n_steps - 1), 0, 0),
                         memory_space=pltpu.SMEM),
            pl.BlockSpec((1, 1, tm * TOP_K), lambda i: (jnp.minimum(i + 2, n_steps - 1), 0, 0),
                         memory_space=pltpu.SMEM),
            pl.BlockSpec((tm, ROUTER_PAD), row),
            pl.BlockSpec((tm, d), row),
            pl.BlockSpec((1, d), const),
            pl.BlockSpec((1, d), const),
            pl.BlockSpec((1, d), const),
            pl.BlockSpec(memory_space=pl.ANY),
        ],
        out_specs=pl.BlockSpec((tm, d), row),
        out_shape=jax.ShapeDtypeStruct((s, d), F32),
        scratch_shapes=[pltpu.VMEM((GATHER_SLOTS * TOP_K * tm * SLAB_PITCH, LANES), U32),
                        pltpu.SemaphoreType.DMA((GATHER_SLOTS,))],
        compiler_params=_cparams(("arbitrary",)),
        name="moe_combine",
    )(pos3, pos3, pos3, tw, x1, g2, ln_g, ln_b, y_sorted)


def _routing_tables(top_i, s):
    tm = TM_MOE
    n_pairs = s * TOP_K
    n_tiles = n_pairs // tm + N_EXPERTS
    flat_e = top_i.reshape(n_pairs)
    onehot = (flat_e[:, None] == jnp.arange(N_EXPERTS, dtype=jnp.int32)[None, :]).astype(jnp.int32)
    csum = jnp.cumsum(onehot, axis=0)
    rank = jnp.sum(csum * onehot, axis=1) - 1
    counts = csum[-1]
    padded = ((counts + tm - 1) // tm) * tm
    ends = jnp.cumsum(padded)
    starts = ends - padded
    pos = jnp.sum(onehot * starts[None, :], axis=1) + rank
    row_token = jnp.zeros((n_tiles * tm,), jnp.int32).at[pos].set(
        (jnp.arange(n_pairs, dtype=jnp.int32) // TOP_K) * SLAB_PITCH,
        unique_indices=True, mode="promise_in_bounds")
    tile_start = jnp.arange(n_tiles, dtype=jnp.int32) * tm
    tile_expert = jnp.sum((tile_start[:, None] >= ends[None, :]).astype(jnp.int32), axis=1)
    tile_expert = jnp.minimum(tile_expert, N_EXPERTS - 1).astype(jnp.int32)
    n_valid = (ends[-1] // tm).astype(jnp.int32).reshape(1)
    run_first = jnp.concatenate([jnp.ones((1,), jnp.int32),
                                 (tile_expert[1:] != tile_expert[:-1]).astype(jnp.int32)])
    run_parity = (jnp.cumsum(run_first) - 1) % 2
    experts = jnp.arange(N_EXPERTS, dtype=jnp.int32)
    present = jnp.where(counts > 0, experts, N_EXPERTS)
    later = jnp.concatenate([lax.cummin(present[::-1])[::-1][1:], jnp.full((1,), N_EXPERTS, jnp.int32)])
    next_expert = jnp.where(later < N_EXPERTS, later, -1)[tile_expert]
    pos_flat = (pos * SLAB_PITCH).astype(jnp.int32)
    tile_tables = (tile_expert, n_valid, run_first.astype(jnp.int32), run_parity.astype(jnp.int32),
                   next_expert.astype(jnp.int32))
    return pos_flat, row_token.reshape(n_tiles, 1, tm), tile_tables


def kernel(x, c, w_ada, b_ada, w_in, b_f, sgu_ln_g, spatial_w, spatial_b, w_proj_a, w_proj_b, w_o,
           ln1_g, ln1_b, router_w, router_b, w_gate, b_gate, w_up, b_up, w_down, b_down, ln2_g, ln2_b):
    bsz, s, d = x.shape
    assert bsz == 1
    depth = w_ada.shape[0]
    xs = x.reshape(s, d)
    mod = _adaln(c.reshape(d, 1), w_ada, b_ada)

    n_a = 2 * A_WIDTH
    n_b = 3 * B_WIDTH
    w_main = jnp.concatenate(
        [w_in[:, :, n_a + n_b + B_HEADS:], w_in[:, :, :n_a + n_b]], axis=-1).astype(BF16)

    def split_pad(w):
        hi = w.astype(BF16)
        lo = (w - hi.astype(F32)).astype(BF16)
        pad = ((0, 0), (0, 0), (0, LANES - w.shape[-1]))
        return jnp.concatenate([jnp.pad(hi, pad), jnp.pad(lo, pad)], axis=-1)

    w_f = split_pad(w_in[:, :, n_a + n_b:n_a + n_b + B_HEADS])
    wa16 = w_proj_a.astype(BF16)
    wb16 = w_proj_b.astype(BF16)
    wo16 = w_o.astype(BF16)
    rw_pad = split_pad(router_w)
    rb_pad = jnp.pad(router_b, ((0, 0), (0, ROUTER_PAD - N_EXPERTS)), constant_values=float("-inf"))

    n_all = depth * N_EXPERTS
    wg_all = w_gate.reshape(n_all, d, D_EXPERT)
    wu_all = w_up.reshape(n_all, d, D_EXPERT)
    wd_all = w_down.reshape(n_all, D_EXPERT, d)
    bg_all = b_gate.reshape(n_all, 1, D_EXPERT)
    bu_all = b_up.reshape(n_all, 1, D_EXPERT)
    bd_all = b_down.reshape(n_all, 1, d)

    for l in range(depth):
        sh1, sc1, g1, sh2, sc2, g2 = [mod[l, :, k * d:(k + 1) * d] for k in range(6)]
        z, f_logit = _inproj(xs, sc1, sh1, w_main[l], w_f[l])
        q_sumsq, k_sumsq = _tile_norms(z)
        cum_hs = _forget_cumsum(f_logit.T, b_f[l].reshape(B_HEADS, 1))
        ya = _sgu(z, sgu_ln_g[l].reshape(1, A_WIDTH), spatial_w[l], spatial_b[l].T)
        yb = _attention(z, cum_hs, _prune_bounds(q_sumsq, k_sumsq, cum_hs))
        x1, h2, top_i, top_w = _post_mixer(
            ya, yb, z, xs, wa16[l], wb16[l], wo16[l], g1, ln1_g[l].reshape(1, d),
            ln1_b[l].reshape(1, d), sc2, sh2, rw_pad[l], rb_pad[l].reshape(1, ROUTER_PAD))
        pos, row_token, tile_tables = _routing_tables(top_i[:, :TOP_K], s)
        y_sorted = _moe_experts(l, tile_tables, row_token, h2, wg_all, bg_all, wu_all, bu_all,
                                wd_all, bd_all)
        xs = _combine(pos, top_w, x1, g2, ln2_g[l].reshape(1, d), ln2_b[l].reshape(1, d), y_sorted)
    return xs.reshape(bsz, s, d)
```

```python
import functools

import jax
import jax.numpy as jnp
from jax import lax
from jax.experimental import pallas as pl
from jax.experimental.pallas import tpu as pltpu

F32 = jnp.float32
BF16 = jnp.bfloat16
HIGHEST = lax.Precision.HIGHEST

D_MODEL = 2048
DEPTH = 4
CHUNK = 128
A_GROUPS = 8
A_WIDTH = 1024
B_HEADS = 8
B_HEAD_DIM = 128
B_WIDTH = 1024
N_EXPERTS = 32
TOP_K = 4
D_EXPERT = 512
SWIGLU_LIMIT = 7.0
SWIGLU_ALPHA = 1.702
DEEPNORM_ALPHA = (2.0 * DEPTH) ** 0.25
LN_EPS = 1e-5
MASK_VALUE = -1e30

LANES = 128
VMEM_LIMIT = 56 * 1024 * 1024

Z_COLS = 2 * D_MODEL + 2 * A_WIDTH + 3 * B_WIDTH
Z_GA, Z_GB, Z_U, Z_V, Z_Q, Z_K, Z_VB = 0, 2048, 4096, 5120, 6144, 7168, 8192

TN_IN = 1024
TN_SUB = 256
TM_IN = 1024
TM_SGU = 256
TQ = 512
PRUNE_MARGIN = 110.0
NORM_SLACK = 1.01
ATTN_ROW_GROUPS = 2
TM_POST = 256
TM_MOE = 256
WEIGHT_CAST_CHUNKS = 8
TM_COMB = 128
CUM_CHUNK = 512
ROUTER_PAD = 128

U32 = jnp.uint32
N_SLABS = D_MODEL // (2 * LANES)
SLAB_PITCH = 9
GATHER_SLOTS = 3


def _bf16_bits(v):
    return lax.bitcast_convert_type(v.astype(BF16).astype(F32), U32)


def _store_slabs(ref, base, rows, value):
    for a in range(N_SLABS):
        hi = _bf16_bits(value[:, a * LANES:(a + 1) * LANES])
        lo = _bf16_bits(value[:, (a + N_SLABS) * LANES:(a + N_SLABS + 1) * LANES])
        ref[pl.ds(base + a, rows, stride=SLAB_PITCH), :] = hi | (lo >> 16)
    zeros = jnp.zeros((rows, LANES), U32)
    for a in range(N_SLABS, SLAB_PITCH):
        ref[pl.ds(base + a, rows, stride=SLAB_PITCH), :] = zeros


def _load_slab(ref, base, rows, a):
    w = ref[pl.ds(base + a, rows, stride=SLAB_PITCH), :]
    hi = lax.bitcast_convert_type(w & jnp.uint32(0xFFFF0000), F32)
    lo = lax.bitcast_convert_type(w << 16, F32)
    return hi, lo


def _split_bf16(v):
    hi = v.astype(BF16)
    lo = (v - hi.astype(F32)).astype(BF16)
    return hi, lo


def _cparams(sem, vmem=VMEM_LIMIT):
    return pltpu.CompilerParams(dimension_semantics=sem, vmem_limit_bytes=vmem)


def _adaln_body(c_ref, w_ref, b_ref, o_ref):
    c = c_ref[...]
    cond = c * jax.nn.sigmoid(c)
    o_ref[0] = jnp.sum(w_ref[0] * cond, axis=0, keepdims=True) + b_ref[0]


def _adaln(c_col, w_ada, b_ada):
    depth, d, n = w_ada.shape
    tn = 1024
    return pl.pallas_call(
        _adaln_body,
        grid=(depth, n // tn),
        in_specs=[
            pl.BlockSpec((d, 1), lambda l, j: (0, 0)),
            pl.BlockSpec((1, d, tn), lambda l, j: (l, 0, j)),
            pl.BlockSpec((1, 1, tn), lambda l, j: (l, 0, j)),
        ],
        out_specs=pl.BlockSpec((1, 1, tn), lambda l, j: (l, 0, j)),
        out_shape=jax.ShapeDtypeStruct((depth, 1, n), F32),
        compiler_params=_cparams(("arbitrary", "arbitrary")),
        name="adaln",
    )(c_col, w_ada, b_ada.reshape(depth, 1, n))


def _gelu_tanh(x):
    return 0.5 * x * (1.0 + jnp.tanh(0.7978845608028654 * (x + 0.044715 * (x * x * x))))


def _inproj_body(x_ref, sc_ref, sh_ref, w_ref, wf_ref, z_ref, f_ref, h_scr):
    j = pl.program_id(1)

    @pl.when(j == 0)
    def _():
        h = x_ref[...] * (1.0 + sc_ref[...]) + sh_ref[...]
        h_hi, h_lo = _split_bf16(h)
        h_scr[...] = h_hi
        r1 = jnp.dot(h_hi, wf_ref[...], preferred_element_type=F32)
        r2 = jnp.dot(h_lo, wf_ref[:, :LANES], preferred_element_type=F32)
        f_ref[...] = (r1[:, :LANES] + r1[:, LANES:] + r2)[:, :B_HEADS]

    n_sig = (2 * D_MODEL) // TN_IN
    n_gelu = (2 * A_WIDTH) // TN_IN
    n_sub = TN_IN // TN_SUB

    def project(epilogue):
        for c in range(n_sub):
            cols = slice(c * TN_SUB, (c + 1) * TN_SUB)
            acc = jnp.dot(h_scr[...], w_ref[:, cols], preferred_element_type=F32)
            z_ref[:, cols] = epilogue(acc).astype(BF16)

    @pl.when(j < n_sig)
    def _():
        project(jax.nn.sigmoid)

    @pl.when(jnp.logical_and(j >= n_sig, j < n_sig + n_gelu))
    def _():
        project(_gelu_tanh)

    @pl.when(j >= n_sig + n_gelu)
    def _():
        project(lambda a: a)


def _inproj(x, sc, sh, w_main, w_f):
    s, d = x.shape
    tm = min(TM_IN, s)
    return pl.pallas_call(
        _inproj_body,
        grid=(s // tm, Z_COLS // TN_IN),
        in_specs=[
            pl.BlockSpec((tm, d), lambda i, j: (i, 0)),
            pl.BlockSpec((1, d), lambda i, j: (0, 0)),
            pl.BlockSpec((1, d), lambda i, j: (0, 0)),
            pl.BlockSpec((d, TN_IN), lambda i, j: (0, j)),
            pl.BlockSpec((d, 2 * LANES), lambda i, j: (0, 0)),
        ],
        out_specs=[
            pl.BlockSpec((tm, TN_IN), lambda i, j: (i, j)),
            pl.BlockSpec((tm, B_HEADS), lambda i, j: (i, 0)),
        ],
        out_shape=[
            jax.ShapeDtypeStruct((s, Z_COLS), BF16),
            jax.ShapeDtypeStruct((s, B_HEADS), F32),
        ],
        scratch_shapes=[pltpu.VMEM((tm, d), BF16)],
        compiler_params=_cparams(("arbitrary", "arbitrary")),
        name="inproj",
    )(x, sc, sh, w_main, w_f)


def _cum_body(f_ref, bf_ref, o_ref):
    n_chunks = f_ref.shape[1] // CUM_CHUNK
    row = lax.broadcasted_iota(jnp.int32, (CUM_CHUNK, CUM_CHUNK), 0)
    col = lax.broadcasted_iota(jnp.int32, (CUM_CHUNK, CUM_CHUNK), 1)
    upper = jnp.where(row <= col, 1.0, 0.0).astype(F32)

    def body(i, carry):
        sl = pl.ds(pl.multiple_of(i * CUM_CHUNK, CUM_CHUNK), CUM_CHUNK)
        xf = f_ref[:, sl] + bf_ref[...]
        logf = jnp.minimum(xf, 0.0) - jnp.log1p(jnp.exp(-jnp.abs(xf)))
        cs = jnp.dot(logf, upper, precision=HIGHEST, preferred_element_type=F32) + carry
        o_ref[:, sl] = cs
        return cs[:, CUM_CHUNK - 1:CUM_CHUNK]

    lax.fori_loop(0, n_chunks, body, jnp.zeros((B_HEADS, 1), F32))


def _forget_cumsum(f_t, b_f_col):
    h, s = f_t.shape
    return pl.pallas_call(
        _cum_body,
        out_shape=jax.ShapeDtypeStruct((h, s), F32),
        compiler_params=pltpu.CompilerParams(vmem_limit_bytes=VMEM_LIMIT),
        name="forget_cumsum",
    )(f_t, b_f_col)


def _sgu_body(u_ref, v_ref, g_ref, w_ref, b_ref, o_ref):
    tm = u_ref.shape[0]
    row = lax.broadcasted_iota(jnp.int32, (CHUNK, CHUNK), 0)
    col = lax.broadcasted_iota(jnp.int32, (CHUNK, CHUNK), 1)
    causal = row >= col
    for g in range(A_GROUPS):
        cols = slice(g * CHUNK, (g + 1) * CHUNK)
        w = jnp.where(causal, w_ref[g], 0.0).astype(BF16)
        bias = b_ref[:, g:g + 1]
        gain = g_ref[:, cols]
        for ch in range(tm // CHUNK):
            rows = slice(ch * CHUNK, (ch + 1) * CHUNK)
            v = v_ref[rows, cols].astype(F32)
            mu = jnp.mean(v, axis=-1, keepdims=True)
            vc = v - mu
            var = jnp.mean(vc * vc, axis=-1, keepdims=True)
            vn = vc * lax.rsqrt(var + LN_EPS) * gain
            mixed = jnp.dot(w, vn.astype(BF16), preferred_element_type=F32) + bias
            o_ref[rows, cols] = (u_ref[rows, cols].astype(F32) * mixed).astype(BF16)


def _sgu(z, ln_g, w_s, b_s_t):
    s = z.shape[0]
    tm = min(TM_SGU, s)
    return pl.pallas_call(
        _sgu_body,
        grid=(s // tm,),
        in_specs=[
            pl.BlockSpec((tm, A_WIDTH), lambda i: (i, Z_U // A_WIDTH)),
            pl.BlockSpec((tm, A_WIDTH), lambda i: (i, Z_V // A_WIDTH)),
            pl.BlockSpec((1, A_WIDTH), lambda i: (0, 0)),
            pl.BlockSpec((A_GROUPS, CHUNK, CHUNK), lambda i: (0, 0, 0)),
            pl.BlockSpec((CHUNK, A_GROUPS), lambda i: (0, 0)),
        ],
        out_specs=pl.BlockSpec((tm, A_WIDTH), lambda i: (i, 0)),
        out_shape=jax.ShapeDtypeStruct((s, A_WIDTH), BF16),
        compiler_params=_cparams(("arbitrary",)),
        name="sgu",
    )(z, z, ln_g, w_s, b_s_t)


def _attn_body(lo_ref, q_ref, k_ref, v_ref, ck_ref, o_ref, m_scr, acc_scr, s_even, s_odd):
    h = pl.program_id(0)
    i = pl.program_id(1)
    tq = q_ref.shape[0]
    tk = tq
    scale = B_HEAD_DIM ** -0.5
    q = q_ref[...]
    c_ref = ck_ref[0, :, pl.ds(pl.multiple_of(i * tq, tq), LANES)][:, 0:1]
    ones = jnp.ones((tk, B_HEAD_DIM), BF16)
    lo = lo_ref[h, i]

    m_scr[...] = jnp.full((tq, 1), MASK_VALUE, F32)
    acc_scr[...] = jnp.zeros((tq, 2 * B_HEAD_DIM), F32)

    def key_slice(j):
        return pl.ds(pl.multiple_of(j * tk, tk), tk)

    def raw_scores(j):
        return lax.dot_general(q, k_ref[key_slice(j), :], (((1,), (1,)), ((), ())),
                               preferred_element_type=F32)

    def consume(s_ref, j, masked):
        ks = key_slice(j)
        v_aug = jnp.concatenate([v_ref[ks, :], ones], axis=1)
        bias = c_ref - ck_ref[0, :, ks]
        rows_per_group = tq // ATTN_ROW_GROUPS
        for g in range(ATTN_ROW_GROUPS):
            rows = slice(g * rows_per_group, (g + 1) * rows_per_group)
            s = s_ref[rows, :] * scale + bias
            if masked:
                r = lax.broadcasted_iota(jnp.int32, (rows_per_group, tk), 0) + g * rows_per_group
                c = lax.broadcasted_iota(jnp.int32, (rows_per_group, tk), 1)
                s = jnp.where(r >= c, s, MASK_VALUE)
            m_prev = m_scr[rows, :]
            m_new = jnp.maximum(m_prev, jnp.max(s, axis=1, keepdims=True))
            p = jnp.exp(s - m_new)
            alpha = jnp.exp(m_prev - m_new)
            acc_scr[rows, :] = alpha * acc_scr[rows, :] + jnp.dot(
                p.astype(BF16), v_aug, preferred_element_type=F32)
            m_scr[rows, :] = m_new

    s_even[...] = raw_scores(lo)

    def body(j, carry):
        even_visit = lax.rem(j - lo, 2) == 0

        @pl.when(even_visit)
        def _():
            s_odd[...] = raw_scores(j + 1)
            consume(s_even, j, False)

        @pl.when(jnp.logical_not(even_visit))
        def _():
            s_even[...] = raw_scores(j + 1)
            consume(s_odd, j, False)

        return carry

    lax.fori_loop(lo, i, body, 0)
    last_even = lax.rem(i - lo, 2) == 0

    @pl.when(last_even)
    def _():
        consume(s_even, i, True)

    @pl.when(jnp.logical_not(last_even))
    def _():
        consume(s_odd, i, True)

    acc = acc_scr[...]
    o_ref[...] = (acc[:, :B_HEAD_DIM] / acc[:, B_HEAD_DIM:]).astype(BF16)


def _attention(z, cum_hs, lo):
    s = z.shape[0]
    tq = min(TQ, s)
    grid_spec = pltpu.PrefetchScalarGridSpec(
        num_scalar_prefetch=1,
        grid=(B_HEADS, s // tq),
        in_specs=[
            pl.BlockSpec((tq, B_HEAD_DIM), lambda h, i, lo: (i, Z_Q // B_HEAD_DIM + h)),
            pl.BlockSpec((s, B_HEAD_DIM), lambda h, i, lo: (0, Z_K // B_HEAD_DIM + h)),
            pl.BlockSpec((s, B_HEAD_DIM), lambda h, i, lo: (0, Z_VB // B_HEAD_DIM + h)),
            pl.BlockSpec((1, 1, s), lambda h, i, lo: (h, 0, 0)),
        ],
        out_specs=pl.BlockSpec((tq, B_HEAD_DIM), lambda h, i, lo: (i, h)),
        scratch_shapes=[
            pltpu.VMEM((tq, 1), F32),
            pltpu.VMEM((tq, 2 * B_HEAD_DIM), F32),
            pltpu.VMEM((tq, tq), F32),
            pltpu.VMEM((tq, tq), F32),
        ],
    )
    return pl.pallas_call(
        _attn_body,
        grid_spec=grid_spec,
        out_shape=jax.ShapeDtypeStruct((s, B_WIDTH), BF16),
        compiler_params=_cparams(("arbitrary", "arbitrary")),
        name="fox_attention",
    )(lo, z, z, z, cum_hs.reshape(B_HEADS, 1, s))


def _tile_norm_body(q_ref, k_ref, qo_ref, ko_ref):
    col = lax.broadcasted_iota(jnp.int32, (B_WIDTH, LANES), 0)
    head = lax.broadcasted_iota(jnp.int32, (B_WIDTH, LANES), 1)
    sel = jnp.where(col // B_HEAD_DIM == head, 1.0, 0.0).astype(BF16)
    for src, dst in ((q_ref, qo_ref), (k_ref, ko_ref)):
        zc = src[...]
        sumsq = jnp.dot(zc * zc, sel, preferred_element_type=F32)
        dst[...] = jnp.broadcast_to(jnp.max(sumsq, axis=0, keepdims=True), dst.shape)


def _tile_norms(z):
    s = z.shape[0]
    tq = min(TQ, s)
    nblk = s // tq
    sub = 8
    out = jax.ShapeDtypeStruct((nblk * sub, LANES), F32)
    q_max, k_max = pl.pallas_call(
        _tile_norm_body,
        grid=(nblk,),
        in_specs=[
            pl.BlockSpec((tq, B_WIDTH), lambda i: (i, Z_Q // B_WIDTH)),
            pl.BlockSpec((tq, B_WIDTH), lambda i: (i, Z_K // B_WIDTH)),
        ],
        out_specs=[pl.BlockSpec((sub, LANES), lambda i: (i, 0)),
                   pl.BlockSpec((sub, LANES), lambda i: (i, 0))],
        out_shape=[out, out],
        compiler_params=_cparams(("arbitrary",)),
        name="tile_norms",
    )(z, z)
    pick = lambda t: t.reshape(nblk, sub, LANES)[:, 0, :B_HEADS]
    return pick(q_max), pick(k_max)


def _prune_bounds(q_sumsq, k_sumsq, cum_hs):
    s = cum_hs.shape[1]
    tq = min(TQ, s)
    nblk = s // tq
    scale = B_HEAD_DIM ** -0.5

    qn = jnp.sqrt(q_sumsq).T
    kn = jnp.sqrt(k_sumsq).T
    kpm = lax.cummax(kn, axis=1)
    c_first = cum_hs[:, ::tq]
    c_last = cum_hs[:, tq - 1::tq]
    dot_bound = scale * qn[:, :, None] * (kpm[:, None, :] + kn[:, :, None]) * NORM_SLACK
    bound = dot_bound + c_first[:, :, None] - c_last[:, None, :]
    tile = jnp.arange(nblk, dtype=jnp.int32)
    skip = jnp.logical_and(bound <= -PRUNE_MARGIN, tile[None, None, :] < tile[None, :, None])
    return jnp.sum(skip.astype(jnp.int32), axis=-1)


def _layer_norm(y, g, b):
    mu = jnp.mean(y, axis=-1, keepdims=True)
    yc = y - mu
    var = jnp.mean(yc * yc, axis=-1, keepdims=True)
    return yc * lax.rsqrt(var + LN_EPS) * g + b


def _post_body(ya_ref, yb_ref, ga_ref, gb_ref, x_ref, wa_ref, wb_ref, wo_ref, g1_ref, lng_ref,
               lnb_ref, sc_ref, sh_ref, rw_ref, rb_ref, x1_ref, h2_ref, ti_ref, tw_ref):
    tm = x_ref.shape[0]
    a = jnp.dot(ya_ref[...], wa_ref[...], preferred_element_type=F32)
    b = jnp.dot(yb_ref[...], wb_ref[...], preferred_element_type=F32)
    merged = ga_ref[...].astype(F32) * a + gb_ref[...].astype(F32) * b
    mix = jnp.dot(merged.astype(BF16), wo_ref[...], preferred_element_type=F32)
    x1 = _layer_norm(DEEPNORM_ALPHA * x_ref[...] + (1.0 + g1_ref[...]) * mix,
                     lng_ref[...], lnb_ref[...])
    x1_ref[...] = x1
    h2 = x1 * (1.0 + sc_ref[...]) + sh_ref[...]
    _store_slabs(h2_ref, 0, tm, h2)
    h_hi, h_lo = _split_bf16(h2)
    r1 = jnp.dot(h_hi, rw_ref[...], preferred_element_type=F32)
    r2 = jnp.dot(h_lo, rw_ref[:, :ROUTER_PAD], preferred_element_type=F32)
    logits = r1[:, :ROUTER_PAD] + r1[:, ROUTER_PAD:] + r2 + rb_ref[...]
    lane = lax.broadcasted_iota(jnp.int32, (tm, ROUTER_PAD), 1)
    lane_f = lane.astype(F32)
    neg_inf = float("-inf")
    work = logits
    vals, idxs = [], []
    for _ in range(TOP_K):
        m = jnp.max(work, axis=1, keepdims=True)
        idx = jnp.min(jnp.where(work == m, lane_f, float(ROUTER_PAD)), axis=1, keepdims=True)
        vals.append(m)
        idxs.append(idx)
        work = jnp.where(lane_f == idx, neg_inf, work)
    exps = [jnp.exp(v - vals[0]) for v in vals]
    denom = exps[0] + exps[1] + exps[2] + exps[3]
    ti = jnp.zeros((tm, ROUTER_PAD), F32)
    tw = jnp.zeros((tm, ROUTER_PAD), F32)
    for k in range(TOP_K):
        ti = jnp.where(lane == k, idxs[k], ti)
        tw = jnp.where(lane == k, exps[k] / denom, tw)
    ti_ref[...] = ti.astype(jnp.int32)
    tw_ref[...] = tw


def _post_mixer(ya, yb, z, x, wa, wb, wo, g1, ln_g, ln_b, sc2, sh2, rw_pad, rb_pad):
    s, d = x.shape
    tm = min(TM_POST, s)
    row = lambda i: (i, 0)
    const = lambda i: (0, 0)
    single = pl.Buffered(1)
    return pl.pallas_call(
        _post_body,
        grid=(s // tm,),
        in_specs=[
            pl.BlockSpec((tm, A_WIDTH), row),
            pl.BlockSpec((tm, B_WIDTH), row),
            pl.BlockSpec((tm, d), lambda i: (i, Z_GA // D_MODEL)),
            pl.BlockSpec((tm, d), lambda i: (i, Z_GB // D_MODEL)),
            pl.BlockSpec((tm, d), row),
            pl.BlockSpec((A_WIDTH, d), const, pipeline_mode=single),
            pl.BlockSpec((B_WIDTH, d), const, pipeline_mode=single),
            pl.BlockSpec((d, d), const, pipeline_mode=single),
            pl.BlockSpec((1, d), const),
            pl.BlockSpec((1, d), const),
            pl.BlockSpec((1, d), const),
            pl.BlockSpec((1, d), const),
            pl.BlockSpec((1, d), const),
            pl.BlockSpec((d, 2 * ROUTER_PAD), const),
            pl.BlockSpec((1, ROUTER_PAD), const),
        ],
        out_specs=[
            pl.BlockSpec((tm, d), row),
            pl.BlockSpec((tm * SLAB_PITCH, LANES), row),
            pl.BlockSpec((tm, ROUTER_PAD), row),
            pl.BlockSpec((tm, ROUTER_PAD), row),
        ],
        out_shape=[
            jax.ShapeDtypeStruct((s, d), F32),
            jax.ShapeDtypeStruct((s * SLAB_PITCH, LANES), U32),
            jax.ShapeDtypeStruct((s, ROUTER_PAD), jnp.int32),
            jax.ShapeDtypeStruct((s, ROUTER_PAD), F32),
        ],
        compiler_params=_cparams(("arbitrary",)),
        name="post_mixer",
    )(ya, yb, z, z, x, wa, wb, wo, g1, ln_g, ln_b, sc2, sh2, rw_pad, rb_pad)


def _slab_copy(src_hbm, dst_vmem, src_row, dst_row, sem):
    return pltpu.make_async_copy(src_hbm.at[pl.ds(src_row, N_SLABS)],
                                 dst_vmem.at[pl.ds(dst_row, N_SLABS)], sem)


def _gather_rows(src_hbm, dst_vmem, idx_ref, first, stop, dst_base, dst_row, sem, unrolled):
    if unrolled:
        for r in range(first, stop):
            _slab_copy(src_hbm, dst_vmem, idx_ref[0, 0, r], dst_base + dst_row(r) * SLAB_PITCH, sem).start()
    else:
        def issue(r, carry):
            _slab_copy(src_hbm, dst_vmem, idx_ref[0, 0, r], dst_base + dst_row(r) * SLAB_PITCH, sem).start()
            return carry

        lax.fori_loop(first, stop, issue, 0)


def _wait_rows(src_hbm, dst_vmem, n, dst_base, sem):
    def drain(r, carry):
        _slab_copy(src_hbm, dst_vmem, 0, dst_base + r * SLAB_PITCH, sem).wait()
        return carry

    lax.fori_loop(0, n, drain, 0, unroll=8)


def _moe_body(layer, te_ref, nv_ref, first_ref, par_ref, nxt_ref, rows0_ref, rows1_ref, rows2_ref, h_hbm,
              wg_hbm, bg_ref, wu_hbm, bu_ref, wd_hbm, bd_ref, y_ref, xbuf, wg32, wu32, wd32, wg16, wu16,
              wd16, sem, wsem):
    i = pl.program_id(0)
    tm = xbuf.shape[0] // (GATHER_SLOTS * SLAB_PITCH)
    slot_rows = tm * SLAB_PITCH
    n_valid = nv_ref[0]
    same_row = lambda r: r

    def weight_copies(expert, wslot):
        row = layer * N_EXPERTS + expert
        return (pltpu.make_async_copy(wg_hbm.at[row], wg32.at[wslot], wsem.at[wslot, 0]),
                pltpu.make_async_copy(wu_hbm.at[row], wu32.at[wslot], wsem.at[wslot, 1]),
                pltpu.make_async_copy(wd_hbm.at[row], wd32.at[wslot], wsem.at[wslot, 2]))

    @pl.when(i == 0)
    def _():
        for cp in weight_copies(te_ref[0], 0):
            cp.start()

    def request(idx_ref, tile, unrolled):
        slot = lax.rem(tile, GATHER_SLOTS)

        @pl.when(tile < n_valid)
        def _():
            _gather_rows(h_hbm, xbuf, idx_ref, 0, tm, slot * slot_rows, same_row, sem.at[slot], unrolled)

    @pl.when(i == 0)
    def _():
        request(rows0_ref, i, False)
        request(rows1_ref, i + 1, False)

    request(rows2_ref, i + 2, True)

    @pl.when(i < n_valid)
    def _():
        @pl.when(first_ref[i] == 1)
        def _():
            wslot = par_ref[i]
            for cp in weight_copies(te_ref[i], wslot):
                cp.wait()
            for src, dst in ((wg32, wg16), (wu32, wu16), (wd32, wd16)):
                step = dst.shape[0] // WEIGHT_CAST_CHUNKS

                def cast_rows(c, carry, src=src, dst=dst, step=step):
                    rows = pl.ds(pl.multiple_of(c * step, step), step)
                    dst[rows, :] = src[wslot, rows, :].astype(BF16)
                    return carry

                lax.fori_loop(0, WEIGHT_CAST_CHUNKS, cast_rows, 0)

            @pl.when(nxt_ref[i] >= 0)
            def _():
                for cp in weight_copies(nxt_ref[i], 1 - wslot):
                    cp.start()

        slot = lax.rem(i, GATHER_SLOTS)
        base = slot * slot_rows
        _wait_rows(h_hbm, xbuf, tm, base, sem.at[slot])
        halves = [_load_slab(xbuf, base, tm, a) for a in range(N_SLABS)]
        xb = jnp.concatenate([hi.astype(BF16) for hi, _ in halves] +
                             [lo.astype(BF16) for _, lo in halves], axis=1)
        g = jnp.minimum(jnp.dot(xb, wg16[...], preferred_element_type=F32) + bg_ref[0], SWIGLU_LIMIT)
        u = jnp.clip(jnp.dot(xb, wu16[...], preferred_element_type=F32) + bu_ref[0],
                     -SWIGLU_LIMIT, SWIGLU_LIMIT)
        act = (g * jax.nn.sigmoid(SWIGLU_ALPHA * g) * (u + 1.0)).astype(BF16)
        y = jnp.dot(act, wd16[...], preferred_element_type=F32) + bd_ref[0]
        _store_slabs(y_ref, 0, tm, y)

    @pl.when(i >= nv_ref[0])
    def _():
        y_ref[...] = jnp.zeros(y_ref.shape, U32)


def _moe_experts(layer, tile_tables, row_token, h2, wg, bg, wu, bu, wd, bd):
    d = D_MODEL
    n_tiles = row_token.shape[0]
    tm = row_token.shape[2]
    tile_expert, n_valid, run_first, run_parity, next_expert = tile_tables
    bmap = lambda i, te, *_: (layer * N_EXPERTS + te[i], 0, 0)
    hbm = pl.BlockSpec(memory_space=pl.ANY)
    grid_spec = pltpu.PrefetchScalarGridSpec(
        num_scalar_prefetch=5,
        grid=(n_tiles,),
        in_specs=[
            pl.BlockSpec((1, 1, tm), lambda i, *_: (i, 0, 0), memory_space=pltpu.SMEM),
            pl.BlockSpec((1, 1, tm), lambda i, *_: (jnp.minimum(i + 1, n_tiles - 1), 0, 0),
                         memory_space=pltpu.SMEM),
            pl.BlockSpec((1, 1, tm), lambda i, *_: (jnp.minimum(i + 2, n_tiles - 1), 0, 0),
                         memory_space=pltpu.SMEM),
            hbm,
            hbm,
            pl.BlockSpec((1, 1, D_EXPERT), bmap),
            hbm,
            pl.BlockSpec((1, 1, D_EXPERT), bmap),
            hbm,
            pl.BlockSpec((1, 1, d), bmap),
        ],
        out_specs=pl.BlockSpec((tm * SLAB_PITCH, LANES), lambda i, *_: (i, 0)),
        scratch_shapes=[
            pltpu.VMEM((GATHER_SLOTS * tm * SLAB_PITCH, LANES), U32),
            pltpu.VMEM((2, d, D_EXPERT), F32),
            pltpu.VMEM((2, d, D_EXPERT), F32),
            pltpu.VMEM((2, D_EXPERT, d), F32),
            pltpu.VMEM((d, D_EXPERT), BF16),
            pltpu.VMEM((d, D_EXPERT), BF16),
            pltpu.VMEM((D_EXPERT, d), BF16),
            pltpu.SemaphoreType.DMA((GATHER_SLOTS,)),
            pltpu.SemaphoreType.DMA((2, 3)),
        ],
    )
    return pl.pallas_call(
        functools.partial(_moe_body, layer),
        grid_spec=grid_spec,
        out_shape=jax.ShapeDtypeStruct((n_tiles * tm * SLAB_PITCH, LANES), U32),
        compiler_params=_cparams(("arbitrary",)),
        name="moe_experts",
    )(tile_expert, n_valid, run_first, run_parity, next_expert, row_token, row_token, row_token, h2,
      wg, bg, wu, bu, wd, bd)


def _combine_body(pos_ref, pos_next_ref, pos_next2_ref, tw_ref, x_ref, g2_ref, lng_ref, lnb_ref, y_hbm,
                  o_ref, ybuf, sem):
    i = pl.program_id(0)
    tm = x_ref.shape[0]
    n_rows = tm * TOP_K
    slot_rows = n_rows * SLAB_PITCH
    k_major = lambda r: (r % TOP_K) * tm + r // TOP_K

    def request(idx_ref, step, unrolled):
        slot = lax.rem(step, GATHER_SLOTS)

        @pl.when(step < pl.num_programs(0))
        def _():
            _gather_rows(y_hbm, ybuf, idx_ref, 0, n_rows, slot * slot_rows, k_major, sem.at[slot], unrolled)

    @pl.when(i == 0)
    def _():
        request(pos_ref, i, False)
        request(pos_next_ref, i + 1, False)

    request(pos_next2_ref, i + 2, True)

    slot = lax.rem(i, GATHER_SLOTS)
    base = slot * slot_rows
    _wait_rows(y_hbm, ybuf, n_rows, base, sem.at[slot])

    tw = tw_ref[...]
    first, second = [], []
    for a in range(N_SLABS):
        acc_hi = acc_lo = None
        for k in range(TOP_K):
            hi, lo = _load_slab(ybuf, base + k * tm * SLAB_PITCH, tm, a)
            wk = tw[:, k:k + 1]
            acc_hi = wk * hi if acc_hi is None else acc_hi + wk * hi
            acc_lo = wk * lo if acc_lo is None else acc_lo + wk * lo
        first.append(acc_hi)
        second.append(acc_lo)
    ffn = jnp.concatenate(first + second, axis=1)
    o_ref[...] = _layer_norm(DEEPNORM_ALPHA * x_ref[...] + (1.0 + g2_ref[...]) * ffn,
                             lng_ref[...], lnb_ref[...])


def _combine(pos, tw, x1, g2, ln_g, ln_b, y_sorted):
    s, d = x1.shape
    tm = min(TM_COMB, s)
    n_steps = s // tm
    row = lambda i: (i, 0)
    const = lambda i: (0, 0)
    pos3 = pos.reshape(n_steps, 1, tm * TOP_K)
    return pl.pallas_call(
        _combine_body,
        grid=(n_steps,),
        in_specs=[
            pl.BlockSpec((1, 1, tm * TOP_K), lambda i: (i, 0, 0), memory_space=pltpu.SMEM),
            pl.BlockSpec((1, 1, tm * TOP_K), lambda i: (jnp.minimum(i + 1, n_steps - 1), 0, 0),
                         memory_space=pltpu.SMEM),
            pl.BlockSpec((1, 1, tm * TOP_K), lambda i: (jnp.minimum(i + 2, n_steps - 1), 0, 0),
                         memory_space=pltpu.SMEM),
            pl.BlockSpec((tm, ROUTER_PAD), row),
            pl.BlockSpec((tm, d), row),
            pl.BlockSpec((1, d), const),
            pl.BlockSpec((1, d), const),
            pl.BlockSpec((1, d), const),
            pl.BlockSpec(memory_space=pl.ANY),
        ],
        out_specs=pl.BlockSpec((tm, d), row),
        out_shape=jax.ShapeDtypeStruct((s, d), F32),
        scratch_shapes=[pltpu.VMEM((GATHER_SLOTS * TOP_K * tm * SLAB_PITCH, LANES), U32),
                        pltpu.SemaphoreType.DMA((GATHER_SLOTS,))],
        compiler_params=_cparams(("arbitrary",)),
        name="moe_combine",
    )(pos3, pos3, pos3, tw, x1, g2, ln_g, ln_b, y_sorted)


def _routing_tables(top_i, s):
    tm = TM_MOE
    n_pairs = s * TOP_K
    n_tiles = n_pairs // tm + N_EXPERTS
    flat_e = top_i.reshape(n_pairs)
    onehot = (flat_e[:, None] == jnp.arange(N_EXPERTS, dtype=jnp.int32)[None, :]).astype(jnp.int32)
    csum = jnp.cumsum(onehot, axis=0)
    rank = jnp.sum(csum * onehot, axis=1) - 1
    counts = csum[-1]
    padded = ((counts + tm - 1) // tm) * tm
    ends = jnp.cumsum(padded)
    starts = ends - padded
    pos = jnp.sum(onehot * starts[None, :], axis=1) + rank
    row_token = jnp.zeros((n_tiles * tm,), jnp.int32).at[pos].set(
        (jnp.arange(n_pairs, dtype=jnp.int32) // TOP_K) * SLAB_PITCH,
        unique_indices=True, mode="promise_in_bounds")
    tile_start = jnp.arange(n_tiles, dtype=jnp.int32) * tm
    tile_expert = jnp.sum((tile_start[:, None] >= ends[None, :]).astype(jnp.int32), axis=1)
    tile_expert = jnp.minimum(tile_expert, N_EXPERTS - 1).astype(jnp.int32)
    n_valid = (ends[-1] // tm).astype(jnp.int32).reshape(1)
    run_first = jnp.concatenate([jnp.ones((1,), jnp.int32),
                                 (tile_expert[1:] != tile_expert[:-1]).astype(jnp.int32)])
    run_parity = (jnp.cumsum(run_first) - 1) % 2
    experts = jnp.arange(N_EXPERTS, dtype=jnp.int32)
    present = jnp.where(counts > 0, experts, N_EXPERTS)
    later = jnp.concatenate([lax.cummin(present[::-1])[::-1][1:], jnp.full((1,), N_EXPERTS, jnp.int32)])
    next_expert = jnp.where(later < N_EXPERTS, later, -1)[tile_expert]
    pos_flat = (pos * SLAB_PITCH).astype(jnp.int32)
    tile_tables = (tile_expert, n_valid, run_first.astype(jnp.int32), run_parity.astype(jnp.int32),
                   next_expert.astype(jnp.int32))
    return pos_flat, row_token.reshape(n_tiles, 1, tm), tile_tables


def kernel(x, c, w_ada, b_ada, w_in, b_f, sgu_ln_g, spatial_w, spatial_b, w_proj_a, w_proj_b, w_o,
           ln1_g, ln1_b, router_w, router_b, w_gate, b_gate, w_up, b_up, w_down, b_down, ln2_g, ln2_b):
    bsz, s, d = x.shape
    assert bsz == 1
    depth = w_ada.shape[0]
    xs = x.reshape(s, d)
    mod = _adaln(c.reshape(d, 1), w_ada, b_ada)

    n_a = 2 * A_WIDTH
    n_b = 3 * B_WIDTH
    w_main = jnp.concatenate(
        [w_in[:, :, n_a + n_b + B_HEADS:], w_in[:, :, :n_a + n_b]], axis=-1).astype(BF16)

    def split_pad(w):
        hi = w.astype(BF16)
        lo = (w - hi.astype(F32)).astype(BF16)
        pad = ((0, 0), (0, 0), (0, LANES - w.shape[-1]))
        return jnp.concatenate([jnp.pad(hi, pad), jnp.pad(lo, pad)], axis=-1)

    w_f = split_pad(w_in[:, :, n_a + n_b:n_a + n_b + B_HEADS])
    wa16 = w_proj_a.astype(BF16)
    wb16 = w_proj_b.astype(BF16)
    wo16 = w_o.astype(BF16)
    rw_pad = split_pad(router_w)
    rb_pad = jnp.pad(router_b, ((0, 0), (0, ROUTER_PAD - N_EXPERTS)), constant_values=float("-inf"))

    n_all = depth * N_EXPERTS
    wg_all = w_gate.reshape(n_all, d, D_EXPERT)
    wu_all = w_up.reshape(n_all, d, D_EXPERT)
    wd_all = w_down.reshape(n_all, D_EXPERT, d)
    bg_all = b_gate.reshape(n_all, 1, D_EXPERT)
    bu_all = b_up.reshape(n_all, 1, D_EXPERT)
    bd_all = b_down.reshape(n_all, 1, d)

    for l in range(depth):
        sh1, sc1, g1, sh2, sc2, g2 = [mod[l, :, k * d:(k + 1) * d] for k in range(6)]
        z, f_logit = _inproj(xs, sc1, sh1, w_main[l], w_f[l])
        q_sumsq, k_sumsq = _tile_norms(z)
        cum_hs = _forget_cumsum(f_logit.T, b_f[l].reshape(B_HEADS, 1))
        ya = _sgu(z, sgu_ln_g[l].reshape(1, A_WIDTH), spatial_w[l], spatial_b[l].T)
        yb = _attention(z, cum_hs, _prune_bounds(q_sumsq, k_sumsq, cum_hs))
        x1, h2, top_i, top_w = _post_mixer(
            ya, yb, z, xs, wa16[l], wb16[l], wo16[l], g1, ln1_g[l].reshape(1, d),
            ln1_b[l].reshape(1, d), sc2, sh2, rw_pad[l], rb_pad[l].reshape(1, ROUTER_PAD))
        pos, row_token, tile_tables = _routing_tables(top_i[:, :TOP_K], s)
        y_sorted = _moe_experts(l, tile_tables, row_token, h2, wg_all, bg_all, wu_all, bu_all,
                                wd_all, bd_all)
        xs = _combine(pos, top_w, x1, g2, ln2_g[l].reshape(1, d), ln2_b[l].reshape(1, d), y_sorted)
    return xs.reshape(bsz, s, d)
```
